```python
import math
import jax, jax.numpy as jnp
from jax import lax
import numpy as np

D_MODEL = 1024
BATCH = 2
SEQ = 16384
DEPTH = 1

HG_HEADS = 4
HG_KEY = 128
HG_VAL = 128
HG_WIDTH = HG_HEADS * HG_VAL
HG_CHUNK = 64
ATT_HEADS = 8
ATT_HEAD_DIM = 64
ATT_WIDTH = ATT_HEADS * ATT_HEAD_DIM
DILATED_PATTERNS = ((128, 1), (512, 4), (2048, 16))
MIX_WIDTH = HG_WIDTH + ATT_WIDTH
NUM_BUCKETS = 32
MAX_DISTANCE = 1024
D_FF = -(-8 * D_MODEL // (3 * 256)) * 256
EPS = 1e-6
IN_SPLITS = (HG_WIDTH, HG_WIDTH, HG_WIDTH, HG_WIDTH, HG_WIDTH, ATT_WIDTH, ATT_WIDTH, ATT_WIDTH)
IN_WIDTH = sum(IN_SPLITS)

kernel_name = 'hybrid_hgrn2_dilated_attn_adaln_encoder'


def rms_norm(x, w):
    xf = x.astype(jnp.float32)
    y = xf * lax.rsqrt(jnp.mean(xf * xf, axis=-1, keepdims=True) + EPS)
    return (y * w.astype(jnp.float32)).astype(x.dtype)


def t5_bucket(rel):
    half = NUM_BUCKETS // 2
    max_exact = half // 2
    ret = jnp.where(rel > 0, half, 0)
    n = jnp.abs(rel)
    nf = jnp.maximum(n, 1).astype(jnp.float32)
    large = max_exact + (jnp.log(nf / max_exact) / math.log(MAX_DISTANCE / max_exact)
                         * (half - max_exact)).astype(jnp.int32)
    large = jnp.minimum(large, half - 1)
    return ret + jnp.where(n < max_exact, n, large)


def hgrn2_scan(q, k, v, log_f):
    B, S, H, N = q.shape
    V = v.shape[-1]
    C = HG_CHUNK
    nc = S // C

    def chunks(t):
        return jnp.moveaxis(t.astype(jnp.float32).reshape(B, nc, C, H, t.shape[-1]), 1, 0)

    lower = jnp.tril(jnp.ones((C, C), dtype=bool))

    def step(state, inp):
        qc, kc, vc, gc = inp
        G = jnp.cumsum(gc, axis=1)
        inter = jnp.einsum('bthn,bhnv->bthv', qc * jnp.exp(G), state)
        diff = G[:, :, None] - G[:, None, :]
        decay = jnp.exp(jnp.where(lower[None, :, :, None, None], diff, -jnp.inf))
        scores = jnp.einsum('bthn,bshn,btshn->bhts', qc, kc, decay)
        intra = jnp.einsum('bhts,bshv->bthv', scores, vc)
        G_last = G[:, -1]
        new_state = (jnp.exp(G_last)[..., None] * state
                     + jnp.einsum('bshn,bshv->bhnv', kc * jnp.exp(G_last[:, None] - G), vc))
        return new_state, inter + intra

    init = jnp.zeros((B, H, N, V), jnp.float32)
    _, out = lax.scan(step, init, (chunks(q), chunks(k), chunks(v), chunks(log_f)))
    return jnp.moveaxis(out, 0, 1).reshape(B, S, H, V)


def hgrn2_mixer(q, f_fwd, f_bwd, i, g, lb, gn_w):
    B, S, _ = q.shape
    qh = q.reshape(B, S, HG_HEADS, HG_KEY)
    vh = i.reshape(B, S, HG_HEADS, HG_VAL)

    def gates(z, lbd):
        z = z.astype(jnp.float32).reshape(B, S, HG_HEADS, HG_KEY)
        lbd = lbd.reshape(HG_HEADS, HG_KEY)
        log_f = jnp.logaddexp(jnp.log(lbd), jnp.log1p(-lbd) + jax.nn.log_sigmoid(z))
        k = (1.0 - lbd) * jax.nn.sigmoid(-z)
        return log_f, k

    logf_f, k_f = gates(f_fwd, lb[0])
    logf_b, k_b = gates(f_bwd, lb[1])
    o_f = hgrn2_scan(qh, k_f, vh, logf_f)
    flip = lambda t: jnp.flip(t, axis=1)
    o_b = flip(hgrn2_scan(flip(qh), flip(k_b), flip(vh), flip(logf_b)))
    o = (o_f + o_b)
    o = rms_norm(o, gn_w.reshape(HG_HEADS, HG_VAL)).reshape(B, S, HG_WIDTH)
    return o * jax.nn.silu(g.astype(jnp.float32))


def dilated_branch(q, k, v, rel_bias, window, dilation):
    B, S, H, Dh = q.shape
    R = window // (2 * dilation)
    M = S // dilation
    nb = -(-M // R)
    Mp = nb * R

    def to_blocks(t):
        t = t.reshape(B, M, dilation, H, Dh)
        t = jnp.pad(t, ((0, 0), (0, Mp - M), (0, 0), (0, 0), (0, 0)))
        return t.reshape(B, nb, R, dilation, H, Dh)

    def neighbours(t):
        tp = jnp.pad(t, ((0, 0), (1, 1), (0, 0), (0, 0), (0, 0), (0, 0)))
        return jnp.concatenate([tp[:, :-2], tp[:, 1:-1], tp[:, 2:]], axis=2)

    qb = to_blocks(q)
    kn = neighbours(to_blocks(k))
    vn = neighbours(to_blocks(v))

    off = (jnp.arange(3 * R)[None, :] - R) - jnp.arange(R)[:, None]
    band = jnp.abs(off) <= R
    kidx = jnp.arange(nb)[:, None] * R + jnp.arange(3 * R)[None, :] - R
    valid = (kidx >= 0) & (kidx < M)
    mask = band[None] & valid[:, None, :]
    bias = jnp.transpose(rel_bias[t5_bucket(off * dilation)], (2, 0, 1)).astype(jnp.float32)

    scale = ATT_HEAD_DIM ** -0.5
    logits = jnp.einsum('bnqrhd,bnkrhd->bnrhqk', qb, kn).astype(jnp.float32) * scale + bias
    logits = jnp.where(mask[None, :, None, None], logits, -jnp.inf)
    mx = jnp.max(logits, axis=-1, keepdims=True)
    p = jnp.exp(logits - mx)
    den = jnp.sum(p, axis=-1, keepdims=True)
    out = jnp.einsum('bnrhqk,bnkrhd->bnqrhd', p / den, vn.astype(jnp.float32))
    lse = (mx + jnp.log(den))[..., 0]
    out = out.reshape(B, Mp, dilation, H, Dh)[:, :M].reshape(B, S, H, Dh)
    lse = jnp.transpose(lse, (0, 1, 4, 2, 3)).reshape(B, Mp, dilation, H)[:, :M].reshape(B, S, H)
    return out, lse


def dilated_attention(q, k, v, rel_bias):
    B, S, _ = q.shape
    qh = q.reshape(B, S, ATT_HEADS, ATT_HEAD_DIM)
    kh = k.reshape(B, S, ATT_HEADS, ATT_HEAD_DIM)
    vh = v.reshape(B, S, ATT_HEADS, ATT_HEAD_DIM)
    outs, lses = [], []
    for window, dilation in DILATED_PATTERNS:
        o, l = dilated_branch(qh, kh, vh, rel_bias, window, dilation)
        outs.append(o)
        lses.append(l)
    w = jax.nn.softmax(jnp.stack(lses, axis=0), axis=0)
    o = jnp.einsum('pbsh,pbshd->bshd', w, jnp.stack(outs, axis=0))
    return o.reshape(B, S, ATT_WIDTH)


def setup_inputs(seed: int = 0) -> dict:
    key = jax.random.key(seed)
    ks = jax.random.split(key, 20)
    f32 = jnp.float32
    nrm = lambda k, shape, s: (jax.random.normal(k, shape, f32) * s)
    gain = lambda k, shape: 1.0 + 0.05 * jax.random.normal(k, shape, f32)
    x = jax.random.normal(ks[0], (BATCH, SEQ, D_MODEL), f32)
    c = jax.random.normal(ks[1], (BATCH, D_MODEL), f32)
    rel_bias = nrm(ks[2], (NUM_BUCKETS, ATT_HEADS), 0.5)
    w_ada = nrm(ks[3], (DEPTH, D_MODEL, 6 * D_MODEL), 0.5 * D_MODEL ** -0.5)
    b_ada = nrm(ks[4], (DEPTH, 6 * D_MODEL), 0.02)
    norm1_w = gain(ks[5], (DEPTH, D_MODEL))
    w_in = nrm(ks[6], (DEPTH, D_MODEL, IN_WIDTH), D_MODEL ** -0.5)
    hg_lower_bound = nrm(ks[7], (DEPTH + 1, 2, HG_WIDTH), 0.5)
    hg_norm_w = gain(ks[8], (DEPTH, HG_WIDTH))
    attn_norm_w = gain(ks[9], (DEPTH, ATT_WIDTH))
    w_out = nrm(ks[10], (DEPTH, MIX_WIDTH, D_MODEL), MIX_WIDTH ** -0.5)
    norm2_w = gain(ks[11], (DEPTH, D_MODEL))
    w_gate = nrm(ks[12], (DEPTH, D_MODEL, D_FF), D_MODEL ** -0.5)
    w_up = nrm(ks[13], (DEPTH, D_MODEL, D_FF), D_MODEL ** -0.5)
    w_down = nrm(ks[14], (DEPTH, D_FF, D_MODEL), D_FF ** -0.5)
    final_norm_w = gain(ks[15], (D_MODEL,))
    return {'x': x, 'c': c, 'rel_bias': rel_bias, 'w_ada': w_ada, 'b_ada': b_ada,
            'norm1_w': norm1_w, 'w_in': w_in, 'hg_lower_bound': hg_lower_bound,
            'hg_norm_w': hg_norm_w, 'attn_norm_w': attn_norm_w, 'w_out': w_out,
            'norm2_w': norm2_w, 'w_gate': w_gate, 'w_up': w_up, 'w_down': w_down,
            'final_norm_w': final_norm_w}


def reference(x, c, rel_bias, w_ada, b_ada, norm1_w, w_in, hg_lower_bound, hg_norm_w,
              attn_norm_w, w_out, norm2_w, w_gate, w_up, w_down, final_norm_w):
    lb_all = jnp.cumsum(jax.nn.softmax(hg_lower_bound.astype(jnp.float32), axis=0), axis=0)
    split_at = [int(v) for v in np.cumsum(IN_SPLITS)[:-1]]
    for l in range(DEPTH):
        mod = jax.nn.silu(c) @ w_ada[l] + b_ada[l]
        shift1, scale1, gate1, shift2, scale2, gate2 = jnp.split(mod[:, None, :], 6, axis=-1)

        h = rms_norm(x, norm1_w[l]) * (1.0 + scale1) + shift1
        z = h @ w_in[l]
        hq, hf_f, hf_b, hi, hg, aq, ak, av = jnp.split(z, split_at, axis=-1)
        y_hg = hgrn2_mixer(hq, hf_f, hf_b, hi, hg, lb_all[l], hg_norm_w[l])
        y_att = rms_norm(dilated_attention(aq, ak, av, rel_bias), attn_norm_w[l])
        mix = jnp.concatenate([y_hg.astype(x.dtype), y_att.astype(x.dtype)], axis=-1) @ w_out[l]
        x = x + gate1 * mix

        h2 = rms_norm(x, norm2_w[l]) * (1.0 + scale2) + shift2
        ffn = (jax.nn.silu(h2 @ w_gate[l]) * (h2 @ w_up[l])) @ w_down[l]
        x = x + gate2 * ffn
    return rms_norm(x, final_norm_w)
```

```python
import functools
import math

import numpy as np
import jax
import jax.numpy as jnp
from jax import lax
from jax.experimental import pallas as pl
from jax.experimental.pallas import tpu as pltpu

F32 = jnp.float32
BF16 = jnp.bfloat16

D_MODEL = 1024
HG_HEADS = 4
HG_KEY = 128
HG_WIDTH = HG_HEADS * HG_KEY
ATT_HEADS = 8
ATT_HEAD_DIM = 64
ATT_WIDTH = ATT_HEADS * ATT_HEAD_DIM
DILATIONS = (1, 4, 16)
HALF_WINDOW = 64
NUM_BUCKETS = 32
MAX_DISTANCE = 1024
D_FF = 2816
EPS = 1e-6
IN_WIDTH = 5 * HG_WIDTH + 3 * ATT_WIDTH
COL_HQ, COL_HFF, COL_HFB, COL_HI, COL_HG, COL_AQ, COL_AK, COL_AV = range(8)

VMEM_LIMIT_BYTES = 56 * 1024 * 1024

TM_IN = 512
TM_OUT = 512
FF_CHUNK = 256
HG_CHUNK = 128
HG_SUB = 32
HG_BLOCK = 512
ATT_TILE = 1024
ATT_GQ = 128
NEG = -1e30


def _cparams(sem):
    return pltpu.CompilerParams(dimension_semantics=sem, vmem_limit_bytes=VMEM_LIMIT_BYTES)


def _ada_kernel(c_ref, w_ref, b_ref, o_ref):
    c = c_ref[...]
    sc = c * (1.0 / (1.0 + jnp.exp(-c)))
    o_ref[...] = jnp.dot(sc, w_ref[...], preferred_element_type=F32,
                         precision=lax.Precision.HIGHEST) + b_ref[...]


def _ada(c_pad, w_ada, b_ada):
    rows, d = c_pad.shape
    n = w_ada.shape[1]
    bn = 1536
    return pl.pallas_call(
        _ada_kernel,
        grid=(n // bn,),
        in_specs=[pl.BlockSpec((rows, d), lambda j: (0, 0)),
                  pl.BlockSpec((d, bn), lambda j: (0, j)),
                  pl.BlockSpec((1, bn), lambda j: (0, j))],
        out_specs=pl.BlockSpec((rows, bn), lambda j: (0, j)),
        out_shape=jax.ShapeDtypeStruct((rows, n), F32),
        compiler_params=_cparams(("arbitrary",)),
        name="ada",
    )(c_pad, w_ada, b_ada)


def _t5_bucket_np(rel):
    half = NUM_BUCKETS // 2
    max_exact = half // 2
    ret = np.where(rel > 0, half, 0)
    n = np.abs(rel)
    nf = np.maximum(n, 1).astype(np.float64)
    large = max_exact + (np.log(nf / max_exact) / math.log(MAX_DISTANCE / max_exact)
                         * (half - max_exact)).astype(np.int64)
    large = np.minimum(large, half - 1)
    return (ret + np.where(n < max_exact, n, large)).astype(np.int32)


def _bucket_tiles(gq):
    gk = gq + 2 * HALF_WINDOW
    off = (np.arange(gk)[None, :] - HALF_WINDOW) - np.arange(gq)[:, None]
    tiles = []
    for d in DILATIONS:
        tiles.append(np.where(np.abs(off) <= HALF_WINDOW, _t5_bucket_np(off * d), -1))
    return np.stack(tiles).astype(np.int32)


def _bias_kernel(rb_ref, bk_ref, o_ref):
    h = pl.program_id(1)
    bk = bk_ref[0]
    acc = jnp.full(bk.shape, NEG, F32)
    for b in range(NUM_BUCKETS):
        acc = jnp.where(bk == b, rb_ref[b, h], acc)
    o_ref[0, 0] = acc


def _bias_tiles(rel_bias, buckets):
    npat, gq, gk = buckets.shape
    return pl.pallas_call(
        _bias_kernel,
        grid=(npat, ATT_HEADS),
        in_specs=[pl.BlockSpec(memory_space=pltpu.SMEM),
                  pl.BlockSpec((1, gq, gk), lambda p, h: (p, 0, 0))],
        out_specs=pl.BlockSpec((1, 1, gq, gk), lambda p, h: (p, h, 0, 0)),
        out_shape=jax.ShapeDtypeStruct((npat, ATT_HEADS, gq, gk), F32),
        compiler_params=_cparams(("arbitrary", "arbitrary")),
        name="bias_tiles",
    )(rel_bias, buckets)


def _inproj_kernel(x_ref, mod_ref, nw_ref, w_ref, z_ref):
    x = x_ref[0]
    ms = jnp.mean(x * x, axis=-1, keepdims=True)
    y = x * lax.rsqrt(ms + EPS) * nw_ref[...]
    shift = mod_ref[0, 0:1, :]
    scale = mod_ref[0, 1:2, :]
    h = (y * (1.0 + scale) + shift).astype(BF16)
    for j in range(IN_WIDTH // 512):
        cols = slice(j * 512, (j + 1) * 512)
        z_ref[0, :, cols] = jnp.dot(h, w_ref[:, cols], preferred_element_type=F32).astype(BF16)


def _inproj(x, mod, norm_w, w_in_bf16):
    b, s, d = x.shape
    return pl.pallas_call(
        _inproj_kernel,
        grid=(b, s // TM_IN),
        in_specs=[pl.BlockSpec((1, TM_IN, d), lambda bi, i: (bi, i, 0)),
                  pl.BlockSpec((1, 6, d), lambda bi, i: (bi, 0, 0)),
                  pl.BlockSpec((1, d), lambda bi, i: (0, 0)),
                  pl.BlockSpec((d, IN_WIDTH), lambda bi, i: (0, 0), pipeline_mode=pl.Buffered(1))],
        out_specs=pl.BlockSpec((1, TM_IN, IN_WIDTH), lambda bi, i: (bi, i, 0)),
        out_shape=jax.ShapeDtypeStruct((b, s, IN_WIDTH), BF16),
        compiler_params=_cparams(("arbitrary", "arbitrary")),
        name="inproj",
    )(x, mod, norm_w, w_in_bf16)


def _hgrn_chunk(q, zf, v, lb, tri, st_ref, head, reverse):
    L = HG_CHUNK
    e = jnp.exp(-jnp.abs(zf))
    s = 1.0 / (1.0 + e)
    es = e * s
    pos = zf >= 0
    sig = jnp.where(pos, s, es)
    nsig = jnp.where(pos, es, s)
    one_m_lb = 1.0 - lb
    logf = jnp.log(lb + one_m_lb * sig)
    k = one_m_lb * nsig

    hi = logf.astype(BF16)
    r1 = logf - hi.astype(F32)
    mid = r1.astype(BF16)
    lo = (r1 - mid.astype(F32)).astype(BF16)
    g = (jnp.dot(tri, hi, preferred_element_type=F32)
         + jnp.dot(tri, mid, preferred_element_type=F32)
         + jnp.dot(tri, lo, preferred_element_type=F32))

    nsub = L // HG_SUB
    ref_off = HG_SUB // 2 if reverse else HG_SUB // 2 - 1
    grefs = [g[i * HG_SUB + ref_off:i * HG_SUB + ref_off + 1, :] for i in range(nsub)]
    gref_full = jnp.concatenate([jnp.broadcast_to(r, (HG_SUB, HG_KEY)) for r in grefs], axis=0)
    qq = (q * jnp.exp(g - gref_full)).astype(BF16)

    rows = []
    for i in range(nsub):
        kk = (k * jnp.exp(grefs[i] - g)).astype(BF16)
        rows.append(lax.dot_general(qq[i * HG_SUB:(i + 1) * HG_SUB], kk,
                                    (((1,), (1,)), ((), ())), preferred_element_type=F32))
    a = jnp.concatenate(rows, axis=0)
    t_idx = lax.broadcasted_iota(jnp.int32, (L, L), 0)
    s_idx = lax.broadcasted_iota(jnp.int32, (L, L), 1)
    causal = (s_idx >= t_idx) if reverse else (s_idx <= t_idx)
    a = jnp.where(causal, a, 0.0).astype(BF16)

    g_end = g[0:1, :] if reverse else g[L - 1:L, :]
    st = st_ref[head]
    qg = (q * jnp.exp(g)).astype(BF16)
    vb = v.astype(BF16)
    o = (jnp.dot(a, vb, preferred_element_type=F32)
         + lax.dot_general(qg, st.astype(BF16), (((1,), (1,)), ((), ())),
                           preferred_element_type=F32))
    kd = (k * jnp.exp(g_end - g)).astype(BF16)
    st_ref[head] = (st * jnp.exp(g_end)
                    + lax.dot_general(vb, kd, (((0,), (0,)), ((), ())), preferred_element_type=F32))
    return o


def _hgrn_kernel(qf_ref, ff_ref, vf_ref, qb_ref, fb_ref, vb_ref, lbp_ref, tri_ref,
                 of_ref, ob_ref, stf_ref, stb_ref):
    @pl.when(pl.program_id(1) == 0)
    def _():
        stf_ref[...] = jnp.zeros_like(stf_ref)
        stb_ref[...] = jnp.zeros_like(stb_ref)

    lbp = lbp_ref[...]
    lbe = jnp.exp(lbp - jnp.max(lbp, axis=0, keepdims=True))
    lb_all = lbe[0] / jnp.sum(lbe, axis=0)

    nchunk = HG_BLOCK // HG_CHUNK

    def body(ci, carry):
        rf = pl.multiple_of(ci * HG_CHUNK, HG_CHUNK)
        rb = pl.multiple_of((nchunk - 1 - ci) * HG_CHUNK, HG_CHUNK)
        for h in range(HG_HEADS):
            cols = slice(h * HG_KEY, (h + 1) * HG_KEY)
            o = _hgrn_chunk(qf_ref[0, pl.ds(rf, HG_CHUNK), cols].astype(F32),
                            ff_ref[0, pl.ds(rf, HG_CHUNK), cols].astype(F32),
                            vf_ref[0, pl.ds(rf, HG_CHUNK), cols].astype(F32),
                            lb_all[0:1, cols], tri_ref[0], stf_ref, h, False)
            of_ref[0, pl.ds(rf, HG_CHUNK), cols] = o.astype(BF16)
            o = _hgrn_chunk(qb_ref[0, pl.ds(rb, HG_CHUNK), cols].astype(F32),
                            fb_ref[0, pl.ds(rb, HG_CHUNK), cols].astype(F32),
                            vb_ref[0, pl.ds(rb, HG_CHUNK), cols].astype(F32),
                            lb_all[1:2, cols], tri_ref[1], stb_ref, h, True)
            ob_ref[0, pl.ds(rb, HG_CHUNK), cols] = o.astype(BF16)
        return carry

    lax.fori_loop(0, nchunk, body, 0)


def _hgrn(z, lb_params, tri):
    b, s, _ = z.shape
    nb = s // HG_BLOCK

    def fwd(col):
        return pl.BlockSpec((1, HG_BLOCK, HG_WIDTH), lambda bi, j: (bi, j, col))

    def bwd(col):
        return pl.BlockSpec((1, HG_BLOCK, HG_WIDTH), lambda bi, j: (bi, nb - 1 - j, col))

    out = jax.ShapeDtypeStruct((b, s, HG_WIDTH), BF16)
    return pl.pallas_call(
        _hgrn_kernel,
        grid=(b, nb),
        in_specs=[fwd(COL_HQ), fwd(COL_HFF), fwd(COL_HI),
                  bwd(COL_HQ), bwd(COL_HFB), bwd(COL_HI),
                  pl.BlockSpec(lb_params.shape, lambda bi, j: (0, 0, 0)),
                  pl.BlockSpec((2, HG_CHUNK, HG_CHUNK), lambda bi, j: (0, 0, 0))],
        out_specs=[pl.BlockSpec((1, HG_BLOCK, HG_WIDTH), lambda bi, j: (bi, j, 0)),
                   pl.BlockSpec((1, HG_BLOCK, HG_WIDTH), lambda bi, j: (bi, nb - 1 - j, 0))],
        out_shape=[out, out],
        scratch_shapes=[pltpu.VMEM((HG_HEADS, HG_KEY, HG_KEY), F32),
                        pltpu.VMEM((HG_HEADS, HG_KEY, HG_KEY), F32)],
        compiler_params=_cparams(("arbitrary", "arbitrary")),
        name="hgrn",
    )(z, z, z, z, z, z, lb_params, tri)


def _attn_kernel(q_ref, kp_ref, kc_ref, kn_ref, vp_ref, vc_ref, vn_ref, bias_ref, o_ref,
                 qf_ref, kf_ref, vf_ref, acc_ref, m_ref, l_ref, *, seq_len):
    T = ATT_TILE
    t0 = pl.program_id(1) * T
    for pr in range(ATT_HEADS // 2):
        pc = slice(pr * 128, (pr + 1) * 128)
        qf_ref[pr] = q_ref[0, :, pc].astype(F32) * (ATT_HEAD_DIM ** -0.5)
        for blk, (k_blk, v_blk) in enumerate(((kp_ref, vp_ref), (kc_ref, vc_ref), (kn_ref, vn_ref))):
            kf_ref[pr, blk * T:(blk + 1) * T, :] = k_blk[0, :, pc].astype(F32)
            vf_ref[pr, blk * T:(blk + 1) * T, :] = v_blk[0, :, pc].astype(F32)

    lane = lax.broadcasted_iota(jnp.int32, (1, 128), 1)
    first = lane < ATT_HEAD_DIM

    for p, d in enumerate(DILATIONS):
        nq = T // d
        gq = min(ATT_GQ, nq)
        gk = gq + 2 * HALF_WINDOW
        jk = lax.broadcasted_iota(jnp.int32, (1, gk), 1)

        for r, mq0 in [(r, g * gq) for r in range(d) for g in range(nq // gq)]:
            qrow = r + d * mq0
            krow = T + qrow - d * HALF_WINDOW
            ktok = t0 + qrow - d * HALF_WINDOW + d * jk
            kvalid = (ktok >= 0) & (ktok < seq_len)

            def pair_body(pair, carry, p=p, d=d, gq=gq, gk=gk, qrow=qrow, krow=krow, kvalid=kvalid):
                qg = qf_ref[pair, pl.ds(qrow, gq, stride=d), :]
                kg = kf_ref[pair, pl.ds(krow, gk, stride=d), :].astype(BF16)
                vg = vf_ref[pair, pl.ds(krow, gk, stride=d), :].astype(BF16)
                pvs, ms, ls = [], [], []
                for hh in range(2):
                    sel = first if hh == 0 else jnp.logical_not(first)
                    qh = jnp.where(sel, qg, 0.0).astype(BF16)
                    sc = lax.dot_general(qh, kg, (((1,), (1,)), ((), ())),
                                         preferred_element_type=F32)
                    sc = sc + bias_ref[p, 2 * pair + hh, 0:gq, 0:gk]
                    sc = jnp.where(kvalid, sc, NEG)
                    mx = jnp.max(sc, axis=-1, keepdims=True)
                    pe = jnp.exp(sc - mx)
                    ls.append(jnp.sum(pe, axis=-1, keepdims=True))
                    ms.append(mx)
                    pvs.append(jnp.dot(pe.astype(BF16), vg, preferred_element_type=F32))
                pv = jnp.where(first, pvs[0], pvs[1])
                mx = jnp.where(first, ms[0], ms[1])
                ln = jnp.where(first, ls[0], ls[1])
                dst = (pair, pl.ds(qrow, gq, stride=d), slice(None))
                if p == 0:
                    acc_ref[dst] = pv
                    m_ref[dst] = mx
                    l_ref[dst] = ln
                else:
                    m_old = m_ref[dst]
                    m_new = jnp.maximum(m_old, mx)
                    a_old = jnp.exp(m_old - m_new)
                    a_new = jnp.exp(mx - m_new)
                    acc_ref[dst] = acc_ref[dst] * a_old + pv * a_new
                    l_ref[dst] = l_ref[dst] * a_old + ln * a_new
                    m_ref[dst] = m_new
                return carry

            lax.fori_loop(0, ATT_HEADS // 2, pair_body, 0)

    for pr in range(ATT_HEADS // 2):
        o_ref[0, :, pr * 128:(pr + 1) * 128] = (acc_ref[pr] / l_ref[pr]).astype(BF16)


def _attn(z, bias):
    b, s, _ = z.shape
    nt = s // ATT_TILE
    blk = (1, ATT_TILE, ATT_WIDTH)

    def cur(col):
        return pl.BlockSpec(blk, lambda bi, i: (bi, i, col))

    def prev(col):
        return pl.BlockSpec(blk, lambda bi, i: (bi, jnp.maximum(i - 1, 0), col))

    def nxt(col):
        return pl.BlockSpec(blk, lambda bi, i: (bi, jnp.minimum(i + 1, nt - 1), col))

    return pl.pallas_call(
        functools.partial(_attn_kernel, seq_len=s),
        grid=(b, nt),
        in_specs=[cur(COL_AQ), prev(COL_AK), cur(COL_AK), nxt(COL_AK),
                  prev(COL_AV), cur(COL_AV), nxt(COL_AV),
                  pl.BlockSpec(bias.shape, lambda bi, i: (0, 0, 0, 0), pipeline_mode=pl.Buffered(1))],
        out_specs=pl.BlockSpec(blk, lambda bi, i: (bi, i, 0)),
        out_shape=jax.ShapeDtypeStruct((b, s, ATT_WIDTH), BF16),
        scratch_shapes=[pltpu.VMEM((ATT_HEADS // 2, ATT_TILE, 128), F32),
                        pltpu.VMEM((ATT_HEADS // 2, 3 * ATT_TILE, 128), F32),
                        pltpu.VMEM((ATT_HEADS // 2, 3 * ATT_TILE, 128), F32),
                        pltpu.VMEM((ATT_HEADS // 2, ATT_TILE, 128), F32),
                        pltpu.VMEM((ATT_HEADS // 2, ATT_TILE, 128), F32),
                        pltpu.VMEM((ATT_HEADS // 2, ATT_TILE, 128), F32)],
        compiler_params=_cparams(("arbitrary", "arbitrary")),
        name="attn",
    )(z, z, z, z, z, z, z, bias)


def _out_kernel(x_ref, of_ref, ob_ref, g_ref, att_ref, mod_ref, gnw_ref, anw_ref, n2w_ref, fnw_ref,
                wo_ref, wg_ref, wu_ref, wd_ref, out_ref, h2_ref, acc_ref):
    x = x_ref[0]
    o = of_ref[0].astype(F32) + ob_ref[0].astype(F32)
    gate = g_ref[0].astype(F32)
    gate = gate * (1.0 / (1.0 + jnp.exp(-gate)))
    parts = []
    for h in range(HG_HEADS):
        cols = slice(h * HG_KEY, (h + 1) * HG_KEY)
        oh = o[:, cols]
        ms = jnp.mean(oh * oh, axis=-1, keepdims=True)
        parts.append(oh * lax.rsqrt(ms + EPS) * gnw_ref[:, cols] * gate[:, cols])
    att = att_ref[0].astype(F32)
    ms = jnp.mean(att * att, axis=-1, keepdims=True)
    parts.append(att * lax.rsqrt(ms + EPS) * anw_ref[...])
    y = jnp.concatenate(parts, axis=-1).astype(BF16)
    mix = jnp.dot(y, wo_ref[...], preferred_element_type=F32)
    x1 = x + mod_ref[0, 2:3, :] * mix

    ms = jnp.mean(x1 * x1, axis=-1, keepdims=True)
    h2 = x1 * lax.rsqrt(ms + EPS) * n2w_ref[...]
    h2_ref[...] = (h2 * (1.0 + mod_ref[0, 4:5, :]) + mod_ref[0, 3:4, :]).astype(BF16)
    acc_ref[...] = jnp.zeros_like(acc_ref)

    def ff_body(j, carry):
        c0 = pl.multiple_of(j * FF_CHUNK, FF_CHUNK)
        h2b = h2_ref[...]
        gt = jnp.dot(h2b, wg_ref[:, pl.ds(c0, FF_CHUNK)], preferred_element_type=F32)
        up = jnp.dot(h2b, wu_ref[:, pl.ds(c0, FF_CHUNK)], preferred_element_type=F32)
        act = (gt * (1.0 / (1.0 + jnp.exp(-gt))) * up).astype(BF16)
        acc_ref[...] += jnp.dot(act, wd_ref[pl.ds(c0, FF_CHUNK), :], preferred_element_type=F32)
        return carry

    lax.fori_loop(0, D_FF // FF_CHUNK, ff_body, 0)
    x2 = x1 + mod_ref[0, 5:6, :] * acc_ref[...]
    ms = jnp.mean(x2 * x2, axis=-1, keepdims=True)
    out_ref[0] = x2 * lax.rsqrt(ms + EPS) * fnw_ref[...]


def _out(x, o_f, o_b, z, att, mod, gn_w, an_w, n2_w, fn_w, wo, wg, wu, wd):
    b, s, d = x.shape

    def tok(width, col=0):
        return pl.BlockSpec((1, TM_OUT, width), lambda bi, i: (bi, i, col))

    def const(shape):
        return pl.BlockSpec(shape, lambda bi, i: (0,) * len(shape), pipeline_mode=pl.Buffered(1))

    return pl.pallas_call(
        _out_kernel,
        grid=(b, s // TM_OUT),
        in_specs=[tok(d), tok(HG_WIDTH), tok(HG_WIDTH), tok(HG_WIDTH, COL_HG), tok(ATT_WIDTH),
                  pl.BlockSpec((1, 6, d), lambda bi, i: (bi, 0, 0)),
                  const((1, HG_WIDTH)), const((1, ATT_WIDTH)), const((1, d)), const((1, d)),
                  const(wo.shape), const(wg.shape), const(wu.shape), const(wd.shape)],
        out_specs=tok(d),
        out_shape=jax.ShapeDtypeStruct((b, s, d), F32),
        scratch_shapes=[pltpu.VMEM((TM_OUT, d), BF16), pltpu.VMEM((TM_OUT, d), F32)],
        compiler_params=_cparams(("arbitrary", "arbitrary")),
        name="out_ffn",
    )(x, o_f, o_b, z, att, mod, gn_w, an_w, n2_w, fn_w, wo, wg, wu, wd)


def _cumsum_matrices():
    t = np.arange(HG_CHUNK)
    fwd = (t[None, :] <= t[:, None]).astype(np.float32)
    bwd = (t[None, :] >= t[:, None]).astype(np.float32)
    return jnp.asarray(np.stack([fwd, bwd]), dtype=BF16)


def kernel(x, c, rel_bias, w_ada, b_ada, norm1_w, w_in, hg_lower_bound, hg_norm_w, attn_norm_w,
           w_out, norm2_w, w_gate, w_up, w_down, final_norm_w):
    b, s, d = x.shape
    assert d == D_MODEL and s % (2 * ATT_TILE) == 0 and w_ada.shape[0] == 1
    c_pad = jnp.zeros((8, d), F32).at[:b].set(c)
    mod = _ada(c_pad, w_ada[0], b_ada[0][None, :])[:b].reshape(b, 6, d)

    bias = _bias_tiles(rel_bias.astype(F32), jnp.asarray(_bucket_tiles(ATT_GQ)))
    z = _inproj(x, mod, norm1_w[0][None, :], w_in[0].astype(BF16))
    o_f, o_b = _hgrn(z, hg_lower_bound.astype(F32), _cumsum_matrices())
    att = _attn(z, bias)
    return _out(x, o_f, o_b, z, att, mod,
                hg_norm_w[0][None, :], attn_norm_w[0][None, :], norm2_w[0][None, :],
                final_norm_w[None, :],
                w_out[0].astype(BF16), w_gate[0].astype(BF16), w_up[0].astype(BF16),
                w_down[0].astype(BF16))
```

```python
import functools
import math

import numpy as np
import jax
import jax.numpy as jnp
from jax import lax
from jax.experimental import pallas as pl
from jax.experimental.pallas import tpu as pltpu

F32 = jnp.float32
BF16 = jnp.bfloat16

D_MODEL = 1024
HG_HEADS = 4
HG_KEY = 128
HG_WIDTH = HG_HEADS * HG_KEY
ATT_HEADS = 8
ATT_HEAD_DIM = 64
ATT_WIDTH = ATT_HEADS * ATT_HEAD_DIM
DILATIONS = (1, 4, 16)
HALF_WINDOW = 64
NUM_BUCKETS = 32
MAX_DISTANCE = 1024
D_FF = 2816
EPS = 1e-6
IN_WIDTH = 5 * HG_WIDTH + 3 * ATT_WIDTH
COL_HQ, COL_HFF, COL_HFB, COL_HI, COL_HG, COL_AQ, COL_AK, COL_AV = range(8)

VMEM_LIMIT_BYTES = 56 * 1024 * 1024

TM_IN = 512
TM_OUT = 512
FF_CHUNK = 256
HG_CHUNK = 128
HG_SUB = 32
HG_BLOCK = 512
ATT_TILE = 1024
ATT_GQ = 128
NEG = -1e30


def _cparams(sem):
    return pltpu.CompilerParams(dimension_semantics=sem, vmem_limit_bytes=VMEM_LIMIT_BYTES)


def _ada_kernel(c_ref, w_ref, b_ref, o_ref):
    c = c_ref[...]
    sc = c * (1.0 / (1.0 + jnp.exp(-c)))
    o_ref[...] = jnp.dot(sc, w_ref[...], preferred_element_type=F32,
                         precision=lax.Precision.HIGHEST) + b_ref[...]


def _ada(c_pad, w_ada, b_ada):
    rows, d = c_pad.shape
    n = w_ada.shape[1]
    bn = 1536
    return pl.pallas_call(
        _ada_kernel,
        grid=(n // bn,),
        in_specs=[pl.BlockSpec((rows, d), lambda j: (0, 0)),
                  pl.BlockSpec((d, bn), lambda j: (0, j)),
                  pl.BlockSpec((1, bn), lambda j: (0, j))],
        out_specs=pl.BlockSpec((rows, bn), lambda j: (0, j)),
        out_shape=jax.ShapeDtypeStruct((rows, n), F32),
        compiler_params=_cparams(("arbitrary",)),
        name="ada",
    )(c_pad, w_ada, b_ada)


def _t5_bucket_np(rel):
    half = NUM_BUCKETS // 2
    max_exact = half // 2
    ret = np.where(rel > 0, half, 0)
    n = np.abs(rel)
    nf = np.maximum(n, 1).astype(np.float64)
    large = max_exact + (np.log(nf / max_exact) / math.log(MAX_DISTANCE / max_exact)
                         * (half - max_exact)).astype(np.int64)
    large = np.minimum(large, half - 1)
    return (ret + np.where(n < max_exact, n, large)).astype(np.int32)


def _bucket_tiles(gq):
    gk = gq + 2 * HALF_WINDOW
    off = (np.arange(gk)[None, :] - HALF_WINDOW) - np.arange(gq)[:, None]
    tiles = []
    for d in DILATIONS:
        tiles.append(np.where(np.abs(off) <= HALF_WINDOW, _t5_bucket_np(off * d), -1))
    return np.stack(tiles).astype(np.int32)


def _bias_kernel(rb_ref, bk_ref, o_ref):
    h = pl.program_id(1)
    bk = bk_ref[0]
    acc = jnp.full(bk.shape, NEG, F32)
    for b in range(NUM_BUCKETS):
        acc = jnp.where(bk == b, rb_ref[b, h], acc)
    o_ref[0, 0] = acc


def _bias_tiles(rel_bias, buckets):
    npat, gq, gk = buckets.shape
    return pl.pallas_call(
        _bias_kernel,
        grid=(npat, ATT_HEADS),
        in_specs=[pl.BlockSpec(memory_space=pltpu.SMEM),
                  pl.BlockSpec((1, gq, gk), lambda p, h: (p, 0, 0))],
        out_specs=pl.BlockSpec((1, 1, gq, gk), lambda p, h: (p, h, 0, 0)),
        out_shape=jax.ShapeDtypeStruct((npat, ATT_HEADS, gq, gk), F32),
        compiler_params=_cparams(("arbitrary", "arbitrary")),
        name="bias_tiles",
    )(rel_bias, buckets)


def _inproj_kernel(x_ref, mod_ref, nw_ref, w_ref, z_ref):
    x = x_ref[0]
    ms = jnp.mean(x * x, axis=-1, keepdims=True)
    y = x * lax.rsqrt(ms + EPS) * nw_ref[...]
    shift = mod_ref[0, 0:1, :]
    scale = mod_ref[0, 1:2, :]
    h = (y * (1.0 + scale) + shift).astype(BF16)
    for j in range(IN_WIDTH // 512):
        cols = slice(j * 512, (j + 1) * 512)
        z_ref[0, :, cols] = jnp.dot(h, w_ref[:, cols], preferred_element_type=F32).astype(BF16)


def _inproj(x, mod, norm_w, w_in_bf16):
    b, s, d = x.shape
    return pl.pallas_call(
        _inproj_kernel,
        grid=(b, s // TM_IN),
        in_specs=[pl.BlockSpec((1, TM_IN, d), lambda bi, i: (bi, i, 0)),
                  pl.BlockSpec((1, 6, d), lambda bi, i: (bi, 0, 0)),
                  pl.BlockSpec((1, d), lambda bi, i: (0, 0)),
                  pl.BlockSpec((d, IN_WIDTH), lambda bi, i: (0, 0), pipeline_mode=pl.Buffered(1))],
        out_specs=pl.BlockSpec((1, TM_IN, IN_WIDTH), lambda bi, i: (bi, i, 0)),
        out_shape=jax.ShapeDtypeStruct((b, s, IN_WIDTH), BF16),
        compiler_params=_cparams(("arbitrary", "arbitrary")),
        name="inproj",
    )(x, mod, norm_w, w_in_bf16)


def _hgrn_chunk(q, zf, v, lb, tri, st_ref, head, reverse):
    L = HG_CHUNK
    e = jnp.exp(-jnp.abs(zf))
    s = 1.0 / (1.0 + e)
    es = e * s
    pos = zf >= 0
    sig = jnp.where(pos, s, es)
    nsig = jnp.where(pos, es, s)
    one_m_lb = 1.0 - lb
    logf = jnp.log(lb + one_m_lb * sig)
    k = one_m_lb * nsig

    hi = logf.astype(BF16)
    r1 = logf - hi.astype(F32)
    mid = r1.astype(BF16)
    lo = (r1 - mid.astype(F32)).astype(BF16)
    g = (jnp.dot(tri, hi, preferred_element_type=F32)
         + jnp.dot(tri, mid, preferred_element_type=F32)
         + jnp.dot(tri, lo, preferred_element_type=F32))

    nsub = L // HG_SUB
    ref_off = HG_SUB // 2 if reverse else HG_SUB // 2 - 1
    grefs = [g[i * HG_SUB + ref_off:i * HG_SUB + ref_off + 1, :] for i in range(nsub)]
    gref_full = jnp.concatenate([jnp.broadcast_to(r, (HG_SUB, HG_KEY)) for r in grefs], axis=0)
    qq = (q * jnp.exp(g - gref_full)).astype(BF16)

    rows = []
    for i in range(nsub):
        kk = (k * jnp.exp(grefs[i] - g)).astype(BF16)
        rows.append(lax.dot_general(qq[i * HG_SUB:(i + 1) * HG_SUB], kk,
                                    (((1,), (1,)), ((), ())), preferred_element_type=F32))
    a = jnp.concatenate(rows, axis=0)
    t_idx = lax.broadcasted_iota(jnp.int32, (L, L), 0)
    s_idx = lax.broadcasted_iota(jnp.int32, (L, L), 1)
    causal = (s_idx >= t_idx) if reverse else (s_idx <= t_idx)
    a = jnp.where(causal, a, 0.0).astype(BF16)

    g_end = g[0:1, :] if reverse else g[L - 1:L, :]
    st = st_ref[head]
    qg = (q * jnp.exp(g)).astype(BF16)
    vb = v.astype(BF16)
    o = (jnp.dot(a, vb, preferred_element_type=F32)
         + lax.dot_general(qg, st.astype(BF16), (((1,), (1,)), ((), ())),
                           preferred_element_type=F32))
    kd = (k * jnp.exp(g_end - g)).astype(BF16)
    st_ref[head] = (st * jnp.exp(g_end)
                    + lax.dot_general(vb, kd, (((0,), (0,)), ((), ())), preferred_element_type=F32))
    return o


def _hgrn_kernel(qf_ref, ff_ref, vf_ref, qb_ref, fb_ref, vb_ref, lbp_ref, tri_ref,
                 of_ref, ob_ref, stf_ref, stb_ref):
    @pl.when(pl.program_id(1) == 0)
    def _():
        stf_ref[...] = jnp.zeros_like(stf_ref)
        stb_ref[...] = jnp.zeros_like(stb_ref)

    lbp = lbp_ref[...]
    lbe = jnp.exp(lbp - jnp.max(lbp, axis=0, keepdims=True))
    lb_all = lbe[0] / jnp.sum(lbe, axis=0)

    nchunk = HG_BLOCK // HG_CHUNK

    def body(ci, carry):
        rf = pl.multiple_of(ci * HG_CHUNK, HG_CHUNK)
        rb = pl.multiple_of((nchunk - 1 - ci) * HG_CHUNK, HG_CHUNK)
        for h in range(HG_HEADS):
            cols = slice(h * HG_KEY, (h + 1) * HG_KEY)
            o = _hgrn_chunk(qf_ref[0, pl.ds(rf, HG_CHUNK), cols].astype(F32),
                            ff_ref[0, pl.ds(rf, HG_CHUNK), cols].astype(F32),
                            vf_ref[0, pl.ds(rf, HG_CHUNK), cols].astype(F32),
                            lb_all[0:1, cols], tri_ref[0], stf_ref, h, False)
            of_ref[0, pl.ds(rf, HG_CHUNK), cols] = o.astype(BF16)
            o = _hgrn_chunk(qb_ref[0, pl.ds(rb, HG_CHUNK), cols].astype(F32),
                            fb_ref[0, pl.ds(rb, HG_CHUNK), cols].astype(F32),
                            vb_ref[0, pl.ds(rb, HG_CHUNK), cols].astype(F32),
                            lb_all[1:2, cols], tri_ref[1], stb_ref, h, True)
            ob_ref[0, pl.ds(rb, HG_CHUNK), cols] = o.astype(BF16)
        return carry

    lax.fori_loop(0, nchunk, body, 0)


def _hgrn(z, lb_params, tri):
    b, s, _ = z.shape
    nb = s // HG_BLOCK

    def fwd(col):
        return pl.BlockSpec((1, HG_BLOCK, HG_WIDTH), lambda bi, j: (bi, j, col))

    def bwd(col):
        return pl.BlockSpec((1, HG_BLOCK, HG_WIDTH), lambda bi, j: (bi, nb - 1 - j, col))

    out = jax.ShapeDtypeStruct((b, s, HG_WIDTH), BF16)
    return pl.pallas_call(
        _hgrn_kernel,
        grid=(b, nb),
        in_specs=[fwd(COL_HQ), fwd(COL_HFF), fwd(COL_HI),
                  bwd(COL_HQ), bwd(COL_HFB), bwd(COL_HI),
                  pl.BlockSpec(lb_params.shape, lambda bi, j: (0, 0, 0)),
                  pl.BlockSpec((2, HG_CHUNK, HG_CHUNK), lambda bi, j: (0, 0, 0))],
        out_specs=[pl.BlockSpec((1, HG_BLOCK, HG_WIDTH), lambda bi, j: (bi, j, 0)),
                   pl.BlockSpec((1, HG_BLOCK, HG_WIDTH), lambda bi, j: (bi, nb - 1 - j, 0))],
        out_shape=[out, out],
        scratch_shapes=[pltpu.VMEM((HG_HEADS, HG_KEY, HG_KEY), F32),
                        pltpu.VMEM((HG_HEADS, HG_KEY, HG_KEY), F32)],
        compiler_params=_cparams(("arbitrary", "arbitrary")),
        name="hgrn",
    )(z, z, z, z, z, z, lb_params, tri)


ATT_PAIRS = ATT_HEADS // 2


def _att_geometry(d):
    nq = ATT_TILE // d
    gq = min(ATT_GQ, nq)
    return nq, gq, gq + 2 * HALF_WINDOW, nq // gq, nq + 2 * HALF_WINDOW


def _attn_kernel(q_ref, kp_ref, kc_ref, kn_ref, vp_ref, vc_ref, vn_ref, bias_ref, o_ref,
                 tq_ref, tk_ref, tv_ref,
                 q1_ref, k1_ref, v1_ref, q4_ref, k4_ref, v4_ref, q16_ref, k16_ref, v16_ref,
                 accn_ref, mn_ref, ln_ref, accp_ref, mp_ref, lp_ref, *, seq_len):
    T = ATT_TILE
    t0 = pl.program_id(1) * T
    q_res = (q1_ref, q4_ref, q16_ref)
    k_res = (k1_ref, k4_ref, k16_ref)
    v_res = (v1_ref, v4_ref, v16_ref)

    def regroup(pair, carry):
        pc = pl.ds(pl.multiple_of(pair * 128, 128), 128)
        tq_ref[...] = q_ref[0, :, pc].astype(F32) * (ATT_HEAD_DIM ** -0.5)
        for blk, (k_blk, v_blk) in enumerate(((kp_ref, vp_ref), (kc_ref, vc_ref), (kn_ref, vn_ref))):
            tk_ref[blk * T:(blk + 1) * T, :] = k_blk[0, :, pc].astype(F32)
            tv_ref[blk * T:(blk + 1) * T, :] = v_blk[0, :, pc].astype(F32)
        for p, d in enumerate(DILATIONS):
            nq, _, _, _, nk = _att_geometry(d)
            for r in range(d):
                q_res[p][pair, r * nq:(r + 1) * nq, :] = tq_ref[pl.ds(r, nq, stride=d), :].astype(BF16)
                krows = pl.ds(T - d * HALF_WINDOW + r, nk, stride=d)
                k_res[p][pair, r * nk:(r + 1) * nk, :] = tk_ref[krows, :].astype(BF16)
                v_res[p][pair, r * nk:(r + 1) * nk, :] = tv_ref[krows, :].astype(BF16)
        return carry

    lax.fori_loop(0, ATT_PAIRS, regroup, 0)

    lane = lax.broadcasted_iota(jnp.int32, (1, 128), 1)
    first = lane < ATT_HEAD_DIM

    for p, d in enumerate(DILATIONS):
        nq, gq, gk, gpr, nk = _att_geometry(d)
        jk = lax.broadcasted_iota(jnp.int32, (1, gk), 1)
        dst = (accn_ref, mn_ref, ln_ref) if p == 0 else (accp_ref, mp_ref, lp_ref)

        def group_body(gi, carry, p=p, d=d, nq=nq, gq=gq, gk=gk, gpr=gpr, nk=nk, jk=jk, dst=dst):
            r = gi // gpr
            g = gi % gpr
            q0 = pl.multiple_of(gi * gq, gq)
            k0 = pl.multiple_of(r * nk + g * gq, 64)
            ktok = t0 + r + d * (g * gq - HALF_WINDOW + jk)
            kvalid = (ktok >= 0) & (ktok < seq_len)
            for pair in range(ATT_PAIRS):
                qg = q_res[p][pair, pl.ds(q0, gq), :]
                kg = k_res[p][pair, pl.ds(k0, gk), :]
                vg = v_res[p][pair, pl.ds(k0, gk), :]
                zero = jnp.zeros_like(qg)
                q2 = jnp.concatenate([jnp.where(first, qg, zero), jnp.where(first, zero, qg)], axis=0)
                sc = lax.dot_general(q2, kg, (((1,), (1,)), ((), ())), preferred_element_type=F32)
                if gq == ATT_GQ:
                    b2 = bias_ref[p, pair]
                else:
                    b2 = jnp.concatenate([bias_ref[p, pair, 0:gq, 0:gk],
                                          bias_ref[p, pair, ATT_GQ:ATT_GQ + gq, 0:gk]], axis=0)
                sc = jnp.where(kvalid, sc + b2, NEG)
                mx = jnp.max(sc, axis=-1, keepdims=True)
                pe = jnp.exp(sc - mx)
                ls = jnp.sum(pe, axis=-1, keepdims=True)
                pv = jnp.dot(pe.astype(BF16), vg, preferred_element_type=F32)
                rows = (pair, pl.ds(q0, gq), slice(None))
                dst[0][rows] = jnp.where(first, pv[0:gq], pv[gq:2 * gq])
                dst[1][rows] = jnp.where(first, mx[0:gq], mx[gq:2 * gq])
                dst[2][rows] = jnp.where(first, ls[0:gq], ls[gq:2 * gq])
            return carry

        lax.fori_loop(0, d * gpr, group_body, 0)

        if p > 0:
            def merge(pair, carry, d=d, nq=nq):
                for r in range(d):
                    nat = (pair, pl.ds(r, nq, stride=d), slice(None))
                    res = (pair, slice(r * nq, (r + 1) * nq), slice(None))
                    m_old = mn_ref[nat]
                    m_pat = mp_ref[res]
                    m_new = jnp.maximum(m_old, m_pat)
                    a_old = jnp.exp(m_old - m_new)
                    a_pat = jnp.exp(m_pat - m_new)
                    accn_ref[nat] = accn_ref[nat] * a_old + accp_ref[res] * a_pat
                    ln_ref[nat] = ln_ref[nat] * a_old + lp_ref[res] * a_pat
                    mn_ref[nat] = m_new
                return carry

            lax.fori_loop(0, ATT_PAIRS, merge, 0)

    for pr in range(ATT_PAIRS):
        o_ref[0, :, pr * 128:(pr + 1) * 128] = (accn_ref[pr] / ln_ref[pr]).astype(BF16)


def _attn(z, bias):
    b, s, _ = z.shape
    nt = s // ATT_TILE
    blk = (1, ATT_TILE, ATT_WIDTH)

    def cur(col):
        return pl.BlockSpec(blk, lambda bi, i: (bi, i, col))

    def prev(col):
        return pl.BlockSpec(blk, lambda bi, i: (bi, jnp.maximum(i - 1, 0), col))

    def nxt(col):
        return pl.BlockSpec(blk, lambda bi, i: (bi, jnp.minimum(i + 1, nt - 1), col))

    return pl.pallas_call(
        functools.partial(_attn_kernel, seq_len=s),
        grid=(b, nt),
        in_specs=[cur(COL_AQ), prev(COL_AK), cur(COL_AK), nxt(COL_AK),
                  prev(COL_AV), cur(COL_AV), nxt(COL_AV),
                  pl.BlockSpec(bias.shape, lambda bi, i: (0, 0, 0, 0), pipeline_mode=pl.Buffered(1))],
        out_specs=pl.BlockSpec(blk, lambda bi, i: (bi, i, 0)),
        out_shape=jax.ShapeDtypeStruct((b, s, ATT_WIDTH), BF16),
        scratch_shapes=(
            [pltpu.VMEM((ATT_TILE, 128), F32), pltpu.VMEM((3 * ATT_TILE, 128), F32),
             pltpu.VMEM((3 * ATT_TILE, 128), F32)]
            + [pltpu.VMEM((ATT_PAIRS, rows, 128), BF16)
               for d in DILATIONS
               for rows in (ATT_TILE, d * _att_geometry(d)[4], d * _att_geometry(d)[4])]
            + [pltpu.VMEM((ATT_PAIRS, ATT_TILE, 128), F32) for _ in range(6)]),
        compiler_params=_cparams(("arbitrary", "arbitrary")),
        name="attn",
    )(z, z, z, z, z, z, z, bias)


def _out_kernel(x_ref, of_ref, ob_ref, g_ref, att_ref, mod_ref, gnw_ref, anw_ref, n2w_ref, fnw_ref,
                wo_ref, wg_ref, wu_ref, wd_ref, out_ref, h2_ref, acc_ref):
    x = x_ref[0]
    o = of_ref[0].astype(F32) + ob_ref[0].astype(F32)
    gate = g_ref[0].astype(F32)
    gate = gate * (1.0 / (1.0 + jnp.exp(-gate)))
    parts = []
    for h in range(HG_HEADS):
        cols = slice(h * HG_KEY, (h + 1) * HG_KEY)
        oh = o[:, cols]
        ms = jnp.mean(oh * oh, axis=-1, keepdims=True)
        parts.append(oh * lax.rsqrt(ms + EPS) * gnw_ref[:, cols] * gate[:, cols])
    att = att_ref[0].astype(F32)
    ms = jnp.mean(att * att, axis=-1, keepdims=True)
    parts.append(att * lax.rsqrt(ms + EPS) * anw_ref[...])
    y = jnp.concatenate(parts, axis=-1).astype(BF16)
    mix = jnp.dot(y, wo_ref[...], preferred_element_type=F32)
    x1 = x + mod_ref[0, 2:3, :] * mix

    ms = jnp.mean(x1 * x1, axis=-1, keepdims=True)
    h2 = x1 * lax.rsqrt(ms + EPS) * n2w_ref[...]
    h2_ref[...] = (h2 * (1.0 + mod_ref[0, 4:5, :]) + mod_ref[0, 3:4, :]).astype(BF16)
    acc_ref[...] = jnp.zeros_like(acc_ref)

    def ff_body(j, carry):
        c0 = pl.multiple_of(j * FF_CHUNK, FF_CHUNK)
        h2b = h2_ref[...]
        gt = jnp.dot(h2b, wg_ref[:, pl.ds(c0, FF_CHUNK)], preferred_element_type=F32)
        up = jnp.dot(h2b, wu_ref[:, pl.ds(c0, FF_CHUNK)], preferred_element_type=F32)
        act = (gt * (1.0 / (1.0 + jnp.exp(-gt))) * up).astype(BF16)
        acc_ref[...] += jnp.dot(act, wd_ref[pl.ds(c0, FF_CHUNK), :], preferred_element_type=F32)
        return carry

    lax.fori_loop(0, D_FF // FF_CHUNK, ff_body, 0)
    x2 = x1 + mod_ref[0, 5:6, :] * acc_ref[...]
    ms = jnp.mean(x2 * x2, axis=-1, keepdims=True)
    out_ref[0] = x2 * lax.rsqrt(ms + EPS) * fnw_ref[...]


def _out(x, o_f, o_b, z, att, mod, gn_w, an_w, n2_w, fn_w, wo, wg, wu, wd):
    b, s, d = x.shape

    def tok(width, col=0):
        return pl.BlockSpec((1, TM_OUT, width), lambda bi, i: (bi, i, col))

    def const(shape):
        return pl.BlockSpec(shape, lambda bi, i: (0,) * len(shape), pipeline_mode=pl.Buffered(1))

    return pl.pallas_call(
        _out_kernel,
        grid=(b, s // TM_OUT),
        in_specs=[tok(d), tok(HG_WIDTH), tok(HG_WIDTH), tok(HG_WIDTH, COL_HG), tok(ATT_WIDTH),
                  pl.BlockSpec((1, 6, d), lambda bi, i: (bi, 0, 0)),
                  const((1, HG_WIDTH)), const((1, ATT_WIDTH)), const((1, d)), const((1, d)),
                  const(wo.shape), const(wg.shape), const(wu.shape), const(wd.shape)],
        out_specs=tok(d),
        out_shape=jax.ShapeDtypeStruct((b, s, d), F32),
        scratch_shapes=[pltpu.VMEM((TM_OUT, d), BF16), pltpu.VMEM((TM_OUT, d), F32)],
        compiler_params=_cparams(("arbitrary", "arbitrary")),
        name="out_ffn",
    )(x, o_f, o_b, z, att, mod, gn_w, an_w, n2_w, fn_w, wo, wg, wu, wd)


def _cumsum_matrices():
    t = np.arange(HG_CHUNK)
    fwd = (t[None, :] <= t[:, None]).astype(np.float32)
    bwd = (t[None, :] >= t[:, None]).astype(np.float32)
    return jnp.asarray(np.stack([fwd, bwd]), dtype=BF16)


def kernel(x, c, rel_bias, w_ada, b_ada, norm1_w, w_in, hg_lower_bound, hg_norm_w, attn_norm_w,
           w_out, norm2_w, w_gate, w_up, w_down, final_norm_w):
    b, s, d = x.shape
    assert d == D_MODEL and s % (2 * ATT_TILE) == 0 and w_ada.shape[0] == 1
    c_pad = jnp.zeros((8, d), F32).at[:b].set(c)
    mod = _ada(c_pad, w_ada[0], b_ada[0][None, :])[:b].reshape(b, 6, d)

    bias = _bias_tiles(rel_bias.astype(F32), jnp.asarray(_bucket_tiles(ATT_GQ)))
    bias = bias.reshape(len(DILATIONS), ATT_PAIRS, 2 * ATT_GQ, ATT_GQ + 2 * HALF_WINDOW)
    z = _inproj(x, mod, norm1_w[0][None, :], w_in[0].astype(BF16))
    o_f, o_b = _hgrn(z, hg_lower_bound.astype(F32), _cumsum_matrices())
    att = _attn(z, bias)
    return _out(x, o_f, o_b, z, att, mod,
                hg_norm_w[0][None, :], attn_norm_w[0][None, :], norm2_w[0][None, :],
                final_norm_w[None, :],
                w_out[0].astype(BF16), w_gate[0].astype(BF16), w_up[0].astype(BF16),
                w_down[0].astype(BF16))
```

```python
import functools
import math

import numpy as np
import jax
import jax.numpy as jnp
from jax import lax
from jax.experimental import pallas as pl
from jax.experimental.pallas import tpu as pltpu

F32 = jnp.float32
BF16 = jnp.bfloat16

D_MODEL = 1024
HG_HEADS = 4
HG_KEY = 128
HG_WIDTH = HG_HEADS * HG_KEY
ATT_HEADS = 8
ATT_HEAD_DIM = 64
ATT_WIDTH = ATT_HEADS * ATT_HEAD_DIM
DILATIONS = (1, 4, 16)
HALF_WINDOW = 64
NUM_BUCKETS = 32
MAX_DISTANCE = 1024
D_FF = 2816
EPS = 1e-6
IN_WIDTH = 5 * HG_WIDTH + 3 * ATT_WIDTH
COL_HQ, COL_HFF, COL_HFB, COL_HI, COL_HG, COL_AQ, COL_AK, COL_AV = range(8)

VMEM_LIMIT_BYTES = 56 * 1024 * 1024

TM_IN = 512
TM_OUT = 512
FF_CHUNK = 256
HG_CHUNK = 128
HG_SUB = 32
HG_BLOCK = 512
ATT_TILE = 1024
ATT_GQ = 128
NEG = -1e30
LOG2E = 1.4426950408889634


def _cparams(sem):
    return pltpu.CompilerParams(dimension_semantics=sem, vmem_limit_bytes=VMEM_LIMIT_BYTES)


def _ada_kernel(c_ref, w_ref, b_ref, o_ref):
    c = c_ref[...]
    sc = c * (1.0 / (1.0 + jnp.exp(-c)))
    o_ref[...] = jnp.dot(sc, w_ref[...], preferred_element_type=F32,
                         precision=lax.Precision.HIGHEST) + b_ref[...]


def _ada(c_pad, w_ada, b_ada):
    rows, d = c_pad.shape
    n = w_ada.shape[1]
    bn = 1536
    return pl.pallas_call(
        _ada_kernel,
        grid=(n // bn,),
        in_specs=[pl.BlockSpec((rows, d), lambda j: (0, 0)),
                  pl.BlockSpec((d, bn), lambda j: (0, j)),
                  pl.BlockSpec((1, bn), lambda j: (0, j))],
        out_specs=pl.BlockSpec((rows, bn), lambda j: (0, j)),
        out_shape=jax.ShapeDtypeStruct((rows, n), F32),
        compiler_params=_cparams(("arbitrary",)),
        name="ada",
    )(c_pad, w_ada, b_ada)


def _t5_bucket_np(rel):
    half = NUM_BUCKETS // 2
    max_exact = half // 2
    ret = np.where(rel > 0, half, 0)
    n = np.abs(rel)
    nf = np.maximum(n, 1).astype(np.float64)
    large = max_exact + (np.log(nf / max_exact) / math.log(MAX_DISTANCE / max_exact)
                         * (half - max_exact)).astype(np.int64)
    large = np.minimum(large, half - 1)
    return (ret + np.where(n < max_exact, n, large)).astype(np.int32)


def _bucket_tiles(gq):
    gk = gq + 2 * HALF_WINDOW
    off = (np.arange(gk)[None, :] - HALF_WINDOW) - np.arange(gq)[:, None]
    tiles = []
    for d in DILATIONS:
        tiles.append(np.where(np.abs(off) <= HALF_WINDOW, _t5_bucket_np(off * d), -1))
    return np.stack(tiles).astype(np.int32)


def _bias_kernel(rb_ref, bk_ref, o_ref):
    h = pl.program_id(1)
    bk = bk_ref[0]
    acc = jnp.full(bk.shape, NEG, F32)
    for b in range(NUM_BUCKETS):
        acc = jnp.where(bk == b, rb_ref[b, h] * LOG2E, acc)
    o_ref[0, 0] = acc


def _bias_tiles(rel_bias, buckets):
    npat, gq, gk = buckets.shape
    return pl.pallas_call(
        _bias_kernel,
        grid=(npat, ATT_HEADS),
        in_specs=[pl.BlockSpec(memory_space=pltpu.SMEM),
                  pl.BlockSpec((1, gq, gk), lambda p, h: (p, 0, 0))],
        out_specs=pl.BlockSpec((1, 1, gq, gk), lambda p, h: (p, h, 0, 0)),
        out_shape=jax.ShapeDtypeStruct((npat, ATT_HEADS, gq, gk), F32),
        compiler_params=_cparams(("arbitrary", "arbitrary")),
        name="bias_tiles",
    )(rel_bias, buckets)


def _inproj_kernel(x_ref, mod_ref, nw_ref, w_ref, z_ref):
    x = x_ref[0]
    ms = jnp.mean(x * x, axis=-1, keepdims=True)
    y = x * lax.rsqrt(ms + EPS) * nw_ref[...]
    shift = mod_ref[0, 0:1, :]
    scale = mod_ref[0, 1:2, :]
    h = (y * (1.0 + scale) + shift).astype(BF16)
    for j in range(IN_WIDTH // 512):
        cols = slice(j * 512, (j + 1) * 512)
        z_ref[0, :, cols] = jnp.dot(h, w_ref[:, cols], preferred_element_type=F32).astype(BF16)


def _inproj(x, mod, norm_w, w_in_bf16):
    b, s, d = x.shape
    return pl.pallas_call(
        _inproj_kernel,
        grid=(b, s // TM_IN),
        in_specs=[pl.BlockSpec((1, TM_IN, d), lambda bi, i: (bi, i, 0)),
                  pl.BlockSpec((1, 6, d), lambda bi, i: (bi, 0, 0)),
                  pl.BlockSpec((1, d), lambda bi, i: (0, 0)),
                  pl.BlockSpec((d, IN_WIDTH), lambda bi, i: (0, 0), pipeline_mode=pl.Buffered(1))],
        out_specs=pl.BlockSpec((1, TM_IN, IN_WIDTH), lambda bi, i: (bi, i, 0)),
        out_shape=jax.ShapeDtypeStruct((b, s, IN_WIDTH), BF16),
        compiler_params=_cparams(("arbitrary", "arbitrary")),
        name="inproj",
    )(x, mod, norm_w, w_in_bf16)


_NT =(((1,), (1,)), ((), ()))
_TN = (((0,), (0,)), ((), ()))


def _hgrn_gates(z, lb, log2_one_m_lb):
    e = jnp.exp(-jnp.abs(z))
    ope = 1.0 + e
    l1 = jnp.log2(ope)
    w = jnp.where(z >= 0, 1.0, e)
    logf = jnp.log2(lb * ope + (1.0 - lb) * w) - l1
    logk = log2_one_m_lb - jnp.maximum(z, 0.0) * LOG2E - l1
    return logf, logk


def _hgrn_decays(q, logf, logk, tri, reverse):
    L = HG_CHUNK
    hi = logf.astype(BF16)
    lo = (logf - hi.astype(F32)).astype(BF16)
    res = jnp.dot(tri, jnp.concatenate([hi, lo], axis=1), preferred_element_type=F32)
    g = res[:, :HG_WIDTH] + res[:, HG_WIDTH:]

    nsub = L // HG_SUB
    ref_off = HG_SUB // 2 if reverse else HG_SUB // 2 - 1
    grefs = [g[i * HG_SUB + ref_off:i * HG_SUB + ref_off + 1, :] for i in range(nsub)]
    gref_full = jnp.concatenate([jnp.broadcast_to(r, (HG_SUB, HG_WIDTH)) for r in grefs], axis=0)
    egref_full = jnp.concatenate([jnp.broadcast_to(jnp.exp2(r), (HG_SUB, HG_WIDTH)) for r in grefs],
                                 axis=0)
    qq32 = q * jnp.exp2(g - gref_full)
    qq = qq32.astype(BF16)
    qg = (qq32 * egref_full).astype(BF16)

    lkg = logk - g
    g_end = g[0:1, :] if reverse else g[L - 1:L, :]
    kk = []
    for i in range(nsub):
        lo_r, hi_r = (i * HG_SUB, L) if reverse else (0, (i + 1) * HG_SUB)
        part = jnp.exp2(lkg[lo_r:hi_r] + grefs[i])
        if hi_r - lo_r == L:
            kd = (part * jnp.exp2(g_end - grefs[i])).astype(BF16)
        pad = jnp.zeros((L - (hi_r - lo_r), HG_WIDTH), BF16)
        pieces = [pad, part.astype(BF16)] if reverse else [part.astype(BF16), pad]
        kk.append(jnp.concatenate(pieces, axis=0) if pad.shape[0] else part.astype(BF16))
    return qq, kk, qg, kd, jnp.exp2(g_end)


def _hgrn_kernel(qf_ref, ff_ref, vf_ref, qb_ref, fb_ref, vb_ref, lbp_ref, tri_ref,
                 of_ref, ob_ref, stf_ref, stb_ref):
    @pl.when(pl.program_id(1) == 0)
    def _():
        stf_ref[...] = jnp.zeros_like(stf_ref)
        stb_ref[...] = jnp.zeros_like(stb_ref)

    lbp = lbp_ref[...]
    lbe = jnp.exp(lbp - jnp.max(lbp, axis=0, keepdims=True))
    lb_all = lbe[0] / jnp.sum(lbe, axis=0)
    log2_one_m_lb = jnp.log2(1.0 - lb_all)

    L = HG_CHUNK
    nchunk = HG_BLOCK // L
    nsub = L // HG_SUB
    t_idx = lax.broadcasted_iota(jnp.int32, (L, L), 0)
    s_idx = lax.broadcasted_iota(jnp.int32, (L, L), 1)
    dirs = ((qf_ref, ff_ref, vf_ref, of_ref, stf_ref, False),
            (qb_ref, fb_ref, vb_ref, ob_ref, stb_ref, True))

    def body(ci, carry):
        rows = (pl.ds(pl.multiple_of(ci * L, L), L),
                pl.ds(pl.multiple_of((nchunk - 1 - ci) * L, L), L))
        gates = [_hgrn_gates(f_ref[0, rows[di], :].astype(F32), lb_all[di:di + 1],
                             log2_one_m_lb[di:di + 1])
                 for di, (_, f_ref, _, _, _, _) in enumerate(dirs)]
        dec = [_hgrn_decays(q_ref[0, rows[di], :].astype(F32), gates[di][0], gates[di][1],
                            tri_ref[di], rev)
               for di, (q_ref, _, _, _, _, rev) in enumerate(dirs)]
        units = [(di, h) for h in range(HG_HEADS) for di in range(2)]
        scores = {}
        for di, h in units:
            cols = slice(h * HG_KEY, (h + 1) * HG_KEY)
            qq, kk = dec[di][0], dec[di][1]
            scores[di, h] = jnp.concatenate(
                [lax.dot_general(qq[i * HG_SUB:(i + 1) * HG_SUB, cols], kk[i][:, cols], _NT,
                                 preferred_element_type=F32) for i in range(nsub)], axis=0)
        for di, h in units:
            cols = slice(h * HG_KEY, (h + 1) * HG_KEY)
            _, _, v_ref, o_ref, st_ref, rev = dirs[di]
            _, _, qg, kd, e_end = dec[di]
            causal = (s_idx >= t_idx) if rev else (s_idx <= t_idx)
            a = jnp.where(causal, scores[di, h], 0.0).astype(BF16)
            vh = v_ref[0, rows[di], cols]
            st = st_ref[h]
            o = (jnp.dot(a, vh, preferred_element_type=F32)
                 + lax.dot_general(qg[:, cols], st.astype(BF16), _NT, preferred_element_type=F32))
            o_ref[0, rows[di], cols] = o.astype(BF16)
            st_ref[h] = (st * e_end[:, cols]
                         + lax.dot_general(vh, kd[:, cols], _TN, preferred_element_type=F32))
        return carry

    lax.fori_loop(0, nchunk, body, 0)


def _hgrn(z, lb_params, tri):
    b, s, _ = z.shape
    nb = s // HG_BLOCK

    def fwd(col):
        return pl.BlockSpec((1, HG_BLOCK, HG_WIDTH), lambda bi, j: (bi, j, col))

    def bwd(col):
        return pl.BlockSpec((1, HG_BLOCK, HG_WIDTH), lambda bi, j: (bi, nb - 1 - j, col))

    out = jax.ShapeDtypeStruct((b, s, HG_WIDTH), BF16)
    return pl.pallas_call(
        _hgrn_kernel,
        grid=(b, nb),
        in_specs=[fwd(COL_HQ), fwd(COL_HFF), fwd(COL_HI),
                  bwd(COL_HQ), bwd(COL_HFB), bwd(COL_HI),
                  pl.BlockSpec(lb_params.shape, lambda bi, j: (0, 0, 0)),
                  pl.BlockSpec((2, HG_CHUNK, HG_CHUNK), lambda bi, j: (0, 0, 0))],
        out_specs=[pl.BlockSpec((1, HG_BLOCK, HG_WIDTH), lambda bi, j: (bi, j, 0)),
                   pl.BlockSpec((1, HG_BLOCK, HG_WIDTH), lambda bi, j: (bi, nb - 1 - j, 0))],
        out_shape=[out, out],
        scratch_shapes=[pltpu.VMEM((HG_HEADS, HG_KEY, HG_KEY), F32),
                        pltpu.VMEM((HG_HEADS, HG_KEY, HG_KEY), F32)],
        compiler_params=_cparams(("arbitrary", "arbitrary")),
        name="hgrn",
    )(z, z, z, z, z, z, lb_params, tri)


ATT_PAIRS = ATT_HEADS // 2


def _att_geometry(d):
    nq = ATT_TILE // d
    gq = min(ATT_GQ, nq)
    return nq, gq, gq + 2 * HALF_WINDOW, nq // gq, nq + 2 * HALF_WINDOW


def _attn_kernel(q_ref, kp_ref, kc_ref, kn_ref, vp_ref, vc_ref, vn_ref, bias_ref, o_ref,
                 tq_ref, tk_ref, tv_ref,
                 q1_ref, k1_ref, v1_ref, q4_ref, k4_ref, v4_ref, q16_ref, k16_ref, v16_ref,
                 accn_ref, mn_ref, ln_ref, accp_ref, mp_ref, lp_ref, *, seq_len):
    T = ATT_TILE
    t0 = pl.program_id(1) * T
    q_res = (q1_ref, q4_ref, q16_ref)
    k_res = (k1_ref, k4_ref, k16_ref)
    v_res = (v1_ref, v4_ref, v16_ref)

    def regroup(pair, carry):
        pc = pl.ds(pl.multiple_of(pair * 128, 128), 128)
        tq_ref[...] = q_ref[0, :, pc].astype(F32) * (ATT_HEAD_DIM ** -0.5 * LOG2E)
        for blk, (k_blk, v_blk) in enumerate(((kp_ref, vp_ref), (kc_ref, vc_ref), (kn_ref, vn_ref))):
            tk_ref[blk * T:(blk + 1) * T, :] = k_blk[0, :, pc].astype(F32)
            tv_ref[blk * T:(blk + 1) * T, :] = v_blk[0, :, pc].astype(F32)
        for p, d in enumerate(DILATIONS):
            nq, _, _, _, nk = _att_geometry(d)
            for r in range(d):
                q_res[p][pair, r * nq:(r + 1) * nq, :] = tq_ref[pl.ds(r, nq, stride=d), :].astype(BF16)
                krows = pl.ds(T - d * HALF_WINDOW + r, nk, stride=d)
                k_res[p][pair, r * nk:(r + 1) * nk, :] = tk_ref[krows, :].astype(BF16)
                v_res[p][pair, r * nk:(r + 1) * nk, :] = tv_ref[krows, :].astype(BF16)
        return carry

    lax.fori_loop(0, ATT_PAIRS, regroup, 0)

    lane = lax.broadcasted_iota(jnp.int32, (1, 128), 1)
    first = lane < ATT_HEAD_DIM

    for p, d in enumerate(DILATIONS):
        nq, gq, gk, gpr, nk = _att_geometry(d)
        jk = lax.broadcasted_iota(jnp.int32, (1, gk), 1)
        dst = (accn_ref, mn_ref, ln_ref) if p == 0 else (accp_ref, mp_ref, lp_ref)

        def group_body(gi, carry, p=p, d=d, nq=nq, gq=gq, gk=gk, gpr=gpr, nk=nk, jk=jk, dst=dst):
            r = gi // gpr
            g = gi % gpr
            q0 = pl.multiple_of(gi * gq, gq)
            k0 = pl.multiple_of(r * nk + g * gq, 64)
            ktok = t0 + r + d * (g * gq - HALF_WINDOW + jk)
            kvalid = (ktok >= 0) & (ktok < seq_len)
            pairs = range(ATT_PAIRS)
            scs = []
            for pair in pairs:
                qg = q_res[p][pair, pl.ds(q0, gq), :]
                kg = k_res[p][pair, pl.ds(k0, gk), :]
                zero = jnp.zeros_like(qg)
                q2 = jnp.concatenate([jnp.where(first, qg, zero), jnp.where(first, zero, qg)], axis=0)
                scs.append(lax.dot_general(q2, kg, _NT, preferred_element_type=F32))
            pes, mxs = [], []
            for pair in pairs:
                if gq == ATT_GQ:
                    b2 = bias_ref[p, pair]
                else:
                    b2 = jnp.concatenate([bias_ref[p, pair, 0:gq, 0:gk],
                                          bias_ref[p, pair, ATT_GQ:ATT_GQ + gq, 0:gk]], axis=0)
                sc = jnp.where(kvalid, scs[pair] + b2, NEG)
                mx = jnp.max(sc, axis=-1, keepdims=True)
                pes.append(jnp.exp2(sc - mx))
                mxs.append(mx)
            for pair in pairs:
                pe = pes[pair]
                ls = jnp.sum(pe, axis=-1, keepdims=True)
                vg = v_res[p][pair, pl.ds(k0, gk), :]
                pv = jnp.dot(pe.astype(BF16), vg, preferred_element_type=F32)
                rows = (pair, pl.ds(q0, gq), slice(None))
                dst[0][rows] = jnp.where(first, pv[0:gq], pv[gq:2 * gq])
                dst[1][rows] = jnp.where(first, mxs[pair][0:gq], mxs[pair][gq:2 * gq])
                dst[2][rows] = jnp.where(first, ls[0:gq], ls[gq:2 * gq])
            return carry

        lax.fori_loop(0, d * gpr, group_body, 0)

        if p > 0:
            def merge(pair, carry, d=d, nq=nq):
                for r in range(d):
                    nat = (pair, pl.ds(r, nq, stride=d), slice(None))
                    res = (pair, slice(r * nq, (r + 1) * nq), slice(None))
                    m_old = mn_ref[nat]
                    m_pat = mp_ref[res]
                    m_new = jnp.maximum(m_old, m_pat)
                    a_old = jnp.exp2(m_old - m_new)
                    a_pat = jnp.exp2(m_pat - m_new)
                    accn_ref[nat] = accn_ref[nat] * a_old + accp_ref[res] * a_pat
                    ln_ref[nat] = ln_ref[nat] * a_old + lp_ref[res] * a_pat
                    mn_ref[nat] = m_new
                return carry

            lax.fori_loop(0, ATT_PAIRS, merge, 0)

    for pr in range(ATT_PAIRS):
        o_ref[0, :, pr * 128:(pr + 1) * 128] = (accn_ref[pr] / ln_ref[pr]).astype(BF16)


def _attn(z, bias):
    b, s, _ = z.shape
    nt = s // ATT_TILE
    blk = (1, ATT_TILE, ATT_WIDTH)

    def cur(col):
        return pl.BlockSpec(blk, lambda bi, i: (bi, i, col))

    def prev(col):
        return pl.BlockSpec(blk, lambda bi, i: (bi, jnp.maximum(i - 1, 0), col))

    def nxt(col):
        return pl.BlockSpec(blk, lambda bi, i: (bi, jnp.minimum(i + 1, nt - 1), col))

    return pl.pallas_call(
        functools.partial(_attn_kernel, seq_len=s),
        grid=(b, nt),
        in_specs=[cur(COL_AQ), prev(COL_AK), cur(COL_AK), nxt(COL_AK),
                  prev(COL_AV), cur(COL_AV), nxt(COL_AV),
                  pl.BlockSpec(bias.shape, lambda bi, i: (0, 0, 0, 0), pipeline_mode=pl.Buffered(1))],
        out_specs=pl.BlockSpec(blk, lambda bi, i: (bi, i, 0)),
        out_shape=jax.ShapeDtypeStruct((b, s, ATT_WIDTH), BF16),
        scratch_shapes=(
            [pltpu.VMEM((ATT_TILE, 128), F32), pltpu.VMEM((3 * ATT_TILE, 128), F32),
             pltpu.VMEM((3 * ATT_TILE, 128), F32)]
            + [pltpu.VMEM((ATT_PAIRS, rows, 128), BF16)
               for d in DILATIONS
               for rows in (ATT_TILE, d * _att_geometry(d)[4], d * _att_geometry(d)[4])]
            + [pltpu.VMEM((ATT_PAIRS, ATT_TILE, 128), F32) for _ in range(6)]),
        compiler_params=_cparams(("arbitrary", "arbitrary")),
        name="attn",
    )(z, z, z, z, z, z, z, bias)


def _silu(t):
    return t * (0.5 + 0.5 * jnp.tanh(0.5 * t))


def _out_kernel(x_ref, of_ref, ob_ref, g_ref, att_ref, mod_ref, gnw_ref, anw_ref, n2w_ref, fnw_ref,
                wo_ref, wg_ref, wu_ref, wd_ref, out_ref, h2_ref, acc_ref, act_ref):
    x = x_ref[0]
    o = of_ref[0].astype(F32) + ob_ref[0].astype(F32)
    gate = _silu(g_ref[0].astype(F32))
    parts = []
    for h in range(HG_HEADS):
        cols = slice(h * HG_KEY, (h + 1) * HG_KEY)
        oh = o[:, cols]
        ms = jnp.mean(oh * oh, axis=-1, keepdims=True)
        parts.append(oh * lax.rsqrt(ms + EPS) * gnw_ref[:, cols] * gate[:, cols])
    att = att_ref[0].astype(F32)
    ms = jnp.mean(att * att, axis=-1, keepdims=True)
    parts.append(att * lax.rsqrt(ms + EPS) * anw_ref[...])
    y = jnp.concatenate(parts, axis=-1).astype(BF16)
    mix = jnp.dot(y, wo_ref[...], preferred_element_type=F32)
    x1 = x + mod_ref[0, 2:3, :] * mix

    ms = jnp.mean(x1 * x1, axis=-1, keepdims=True)
    h2 = x1 * lax.rsqrt(ms + EPS) * n2w_ref[...]
    h2_ref[...] = (h2 * (1.0 + mod_ref[0, 4:5, :]) + mod_ref[0, 3:4, :]).astype(BF16)

    nchunks = D_FF // FF_CHUNK

    def act_chunk(j):
        c0 = pl.multiple_of(j * FF_CHUNK, FF_CHUNK)
        h2b = h2_ref[...]
        gt = jnp.dot(h2b, wg_ref[:, pl.ds(c0, FF_CHUNK)], preferred_element_type=F32)
        up = jnp.dot(h2b, wu_ref[:, pl.ds(c0, FF_CHUNK)], preferred_element_type=F32)
        return gt, up

    def down(j):
        c0 = pl.multiple_of(j * FF_CHUNK, FF_CHUNK)
        return jnp.dot(act_ref[j % 2], wd_ref[pl.ds(c0, FF_CHUNK), :], preferred_element_type=F32)

    gt, up = act_chunk(0)
    act_ref[0] = (_silu(gt) * up).astype(BF16)
    acc_ref[...] = jnp.zeros_like(acc_ref)

    def ff_body(j, carry):
        gt, up = act_chunk(j + 1)
        acc_ref[...] += down(j)
        act_ref[(j + 1) % 2] = (_silu(gt) * up).astype(BF16)
        return carry

    lax.fori_loop(0, nchunks - 1, ff_body, 0, unroll=True)
    x2 = x1 + mod_ref[0, 5:6, :] * (acc_ref[...] + down(nchunks - 1))
    ms = jnp.mean(x2 * x2, axis=-1, keepdims=True)
    out_ref[0] = x2 * lax.rsqrt(ms + EPS) * fnw_ref[...]


def _out(x, o_f, o_b, z, att, mod, gn_w, an_w, n2_w, fn_w, wo, wg, wu, wd):
    b, s, d = x.shape

    def tok(width, col=0):
        return pl.BlockSpec((1, TM_OUT, width), lambda bi, i: (bi, i, col))

    def const(shape):
        return pl.BlockSpec(shape, lambda bi, i: (0,) * len(shape), pipeline_mode=pl.Buffered(1))

    return pl.pallas_call(
        _out_kernel,
        grid=(b, s // TM_OUT),
        in_specs=[tok(d), tok(HG_WIDTH), tok(HG_WIDTH), tok(HG_WIDTH, COL_HG), tok(ATT_WIDTH),
                  pl.BlockSpec((1, 6, d), lambda bi, i: (bi, 0, 0)),
                  const((1, HG_WIDTH)), const((1, ATT_WIDTH)), const((1, d)), const((1, d)),
                  const(wo.shape), const(wg.shape), const(wu.shape), const(wd.shape)],
        out_specs=tok(d),
        out_shape=jax.ShapeDtypeStruct((b, s, d), F32),
        scratch_shapes=[pltpu.VMEM((TM_OUT, d), BF16), pltpu.VMEM((TM_OUT, d), F32),
                        pltpu.VMEM((2, TM_OUT, FF_CHUNK), BF16)],
        compiler_params=_cparams(("arbitrary", "arbitrary")),
        name="out_ffn",
    )(x, o_f, o_b, z, att, mod, gn_w, an_w, n2_w, fn_w, wo, wg, wu, wd)


def _cumsum_matrices():
    t = np.arange(HG_CHUNK)
    fwd = (t[None, :] <= t[:, None]).astype(np.float32)
    bwd = (t[None, :] >= t[:, None]).astype(np.float32)
    return jnp.asarray(np.stack([fwd, bwd]), dtype=BF16)


def kernel(x, c, rel_bias, w_ada, b_ada, norm1_w, w_in, hg_lower_bound, hg_norm_w, attn_norm_w,
           w_out, norm2_w, w_gate, w_up, w_down, final_norm_w):
    b, s, d = x.shape
    assert d == D_MODEL and s % (2 * ATT_TILE) == 0 and w_ada.shape[0] == 1
    c_pad = jnp.zeros((8, d), F32).at[:b].set(c)
    mod = _ada(c_pad, w_ada[0], b_ada[0][None, :])[:b].reshape(b, 6, d)

    bias = _bias_tiles(rel_bias.astype(F32), jnp.asarray(_bucket_tiles(ATT_GQ)))
    bias = bias.reshape(len(DILATIONS), ATT_PAIRS, 2 * ATT_GQ, ATT_GQ + 2 * HALF_WINDOW)
    z = _inproj(x, mod, norm1_w[0][None, :], w_in[0].astype(BF16))
    o_f, o_b = _hgrn(z, hg_lower_bound.astype(F32), _cumsum_matrices())
    att = _attn(z, bias)
    return _out(x, o_f, o_b, z, att, mod,
                hg_norm_w[0][None, :], attn_norm_w[0][None, :], norm2_w[0][None, :],
                final_norm_w[None, :],
                w_out[0].astype(BF16), w_gate[0].astype(BF16), w_up[0].astype(BF16),
                w_down[0].astype(BF16))
```

```python
import functools
import math

import numpy as np
import jax
import jax.numpy as jnp
from jax import lax
from jax.experimental import pallas as pl
from jax.experimental.pallas import tpu as pltpu

F32 = jnp.float32
BF16 = jnp.bfloat16

D_MODEL = 1024
HG_HEADS = 4
HG_KEY = 128
HG_WIDTH = HG_HEADS * HG_KEY
ATT_HEADS = 8
ATT_HEAD_DIM = 64
ATT_WIDTH = ATT_HEADS * ATT_HEAD_DIM
DILATIONS = (1, 4, 16)
HALF_WINDOW = 64
NUM_BUCKETS = 32
MAX_DISTANCE = 1024
D_FF = 2816
EPS = 1e-6
IN_WIDTH = 5 * HG_WIDTH + 3 * ATT_WIDTH
COL_HQ, COL_HFF, COL_HFB, COL_HI, COL_HG, COL_AQ, COL_AK, COL_AV = range(8)

VMEM_LIMIT_BYTES = 56 * 1024 * 1024

TM_IN = 512
TM_OUT = 512
FF_CHUNK = 256
HG_CHUNK = 128
HG_SUB = 32
HG_BLOCK = 512
ATT_TILE = 2048
ATT_HALO = HALF_WINDOW * max(DILATIONS)
ATT_GQ = 128
ATT_TRIP_QUERIES = 1024
NEG = -1e30
LOG2E = 1.4426950408889634


def _cparams(sem):
    return pltpu.CompilerParams(dimension_semantics=sem, vmem_limit_bytes=VMEM_LIMIT_BYTES)


def _ada_kernel(c_ref, w_ref, b_ref, o_ref):
    c = c_ref[...]
    sc = c * (1.0 / (1.0 + jnp.exp(-c)))
    o_ref[...] = jnp.dot(sc, w_ref[...], preferred_element_type=F32,
                         precision=lax.Precision.HIGHEST) + b_ref[...]


def _ada(c_pad, w_ada, b_ada):
    rows, d = c_pad.shape
    n = w_ada.shape[1]
    bn = 1536
    return pl.pallas_call(
        _ada_kernel,
        grid=(n // bn,),
        in_specs=[pl.BlockSpec((rows, d), lambda j: (0, 0)),
                  pl.BlockSpec((d, bn), lambda j: (0, j)),
                  pl.BlockSpec((1, bn), lambda j: (0, j))],
        out_specs=pl.BlockSpec((rows, bn), lambda j: (0, j)),
        out_shape=jax.ShapeDtypeStruct((rows, n), F32),
        compiler_params=_cparams(("arbitrary",)),
        name="ada",
    )(c_pad, w_ada, b_ada)


def _t5_bucket_np(rel):
    half = NUM_BUCKETS // 2
    max_exact = half // 2
    ret = np.where(rel > 0, half, 0)
    n = np.abs(rel)
    nf = np.maximum(n, 1).astype(np.float64)
    large = max_exact + (np.log(nf / max_exact) / math.log(MAX_DISTANCE / max_exact)
                         * (half - max_exact)).astype(np.int64)
    large = np.minimum(large, half - 1)
    return (ret + np.where(n < max_exact, n, large)).astype(np.int32)


def _bucket_tiles(gq):
    gk = gq + 2 * HALF_WINDOW
    off = (np.arange(gk)[None, :] - HALF_WINDOW) - np.arange(gq)[:, None]
    tiles = []
    for d in DILATIONS:
        tiles.append(np.where(np.abs(off) <= HALF_WINDOW, _t5_bucket_np(off * d), -1))
    return np.stack(tiles).astype(np.int32)


def _bias_kernel(rb_ref, bk_ref, o_ref):
    bk = bk_ref[0]
    for h in range(ATT_HEADS):
        acc = jnp.full(bk.shape, NEG, F32)
        for b in range(NUM_BUCKETS):
            acc = jnp.where(bk == b, rb_ref[b, h] * LOG2E, acc)
        o_ref[0, h] = acc


def _bias_tiles(rel_bias, buckets):
    npat, gq, gk = buckets.shape
    return pl.pallas_call(
        _bias_kernel,
        grid=(npat,),
        in_specs=[pl.BlockSpec(memory_space=pltpu.SMEM),
                  pl.BlockSpec((1, gq, gk), lambda p: (p, 0, 0))],
        out_specs=pl.BlockSpec((1, ATT_HEADS, gq, gk), lambda p: (p, 0, 0, 0)),
        out_shape=jax.ShapeDtypeStruct((npat, ATT_HEADS, gq, gk), F32),
        compiler_params=_cparams(("arbitrary",)),
        name="bias_tiles",
    )(rel_bias, buckets)


def _inproj_kernel(x_ref, mod_ref, nw_ref, w_ref, z_ref):
    x = x_ref[0]
    ms = jnp.mean(x * x, axis=-1, keepdims=True)
    y = x * lax.rsqrt(ms + EPS) * nw_ref[...]
    shift = mod_ref[0, 0:1, :]
    scale = mod_ref[0, 1:2, :]
    h = (y * (1.0 + scale) + shift).astype(BF16)
    for j in range(IN_WIDTH // 512):
        cols = slice(j * 512, (j + 1) * 512)
        z_ref[0, :, cols] = jnp.dot(h, w_ref[:, cols], preferred_element_type=F32).astype(BF16)


def _inproj(x, mod, norm_w, w_in_bf16):
    b, s, d = x.shape
    return pl.pallas_call(
        _inproj_kernel,
        grid=(b, s // TM_IN),
        in_specs=[pl.BlockSpec((1, TM_IN, d), lambda bi, i: (bi, i, 0)),
                  pl.BlockSpec((1, 6, d), lambda bi, i: (bi, 0, 0)),
                  pl.BlockSpec((1, d), lambda bi, i: (0, 0)),
                  pl.BlockSpec((d, IN_WIDTH), lambda bi, i: (0, 0), pipeline_mode=pl.Buffered(1))],
        out_specs=pl.BlockSpec((1, TM_IN, IN_WIDTH), lambda bi, i: (bi, i, 0)),
        out_shape=jax.ShapeDtypeStruct((b, s, IN_WIDTH), BF16),
        compiler_params=_cparams(("arbitrary", "arbitrary")),
        name="inproj",
    )(x, mod, norm_w, w_in_bf16)


_NT =(((1,), (1,)), ((), ()))
_TN = (((0,), (0,)), ((), ()))


def _hgrn_gates(z, lb, log2_one_m_lb):
    e = jnp.exp(-jnp.abs(z))
    ope = 1.0 + e
    l1 = jnp.log2(ope)
    w = jnp.where(z >= 0, 1.0, e)
    logf = jnp.log2(lb * ope + (1.0 - lb) * w) - l1
    logk = log2_one_m_lb - jnp.maximum(z, 0.0) * LOG2E - l1
    return logf, logk


def _hgrn_decays(q, logf, logk, tri, reverse):
    L = HG_CHUNK
    hi = logf.astype(BF16)
    lo = (logf - hi.astype(F32)).astype(BF16)
    res = jnp.dot(tri, jnp.concatenate([hi, lo], axis=1), preferred_element_type=F32)
    g = res[:, :HG_WIDTH] + res[:, HG_WIDTH:]

    nsub = L // HG_SUB
    ref_off = HG_SUB // 2 if reverse else HG_SUB // 2 - 1
    grefs = [g[i * HG_SUB + ref_off:i * HG_SUB + ref_off + 1, :] for i in range(nsub)]
    gref_full = jnp.concatenate([jnp.broadcast_to(r, (HG_SUB, HG_WIDTH)) for r in grefs], axis=0)
    egref_full = jnp.concatenate([jnp.broadcast_to(jnp.exp2(r), (HG_SUB, HG_WIDTH)) for r in grefs],
                                 axis=0)
    qq32 = q * jnp.exp2(g - gref_full)
    qq = qq32.astype(BF16)
    qg = (qq32 * egref_full).astype(BF16)

    lkg = logk - g
    g_end = g[0:1, :] if reverse else g[L - 1:L, :]
    kk = []
    for i in range(nsub):
        lo_r, hi_r = (i * HG_SUB, L) if reverse else (0, (i + 1) * HG_SUB)
        part = jnp.exp2(lkg[lo_r:hi_r] + grefs[i])
        if hi_r - lo_r == L:
            kd = (part * jnp.exp2(g_end - grefs[i])).astype(BF16)
        pad = jnp.zeros((L - (hi_r - lo_r), HG_WIDTH), BF16)
        pieces = [pad, part.astype(BF16)] if reverse else [part.astype(BF16), pad]
        kk.append(jnp.concatenate(pieces, axis=0) if pad.shape[0] else part.astype(BF16))
    return qq, kk, qg, kd, jnp.exp2(g_end)


def _hgrn_kernel(qf_ref, ff_ref, vf_ref, qb_ref, fb_ref, vb_ref, lbp_ref, tri_ref,
                 of_ref, ob_ref, stf_ref, stb_ref):
    @pl.when(pl.program_id(1) == 0)
    def _():
        stf_ref[...] = jnp.zeros_like(stf_ref)
        stb_ref[...] = jnp.zeros_like(stb_ref)

    lbp = lbp_ref[...]
    lbe = jnp.exp(lbp - jnp.max(lbp, axis=0, keepdims=True))
    lb_all = lbe[0] / jnp.sum(lbe, axis=0)
    log2_one_m_lb = jnp.log2(1.0 - lb_all)

    L = HG_CHUNK
    nchunk = HG_BLOCK // L
    nsub = L // HG_SUB
    t_idx = lax.broadcasted_iota(jnp.int32, (L, L), 0)
    s_idx = lax.broadcasted_iota(jnp.int32, (L, L), 1)
    dirs = ((qf_ref, ff_ref, vf_ref, of_ref, stf_ref, False),
            (qb_ref, fb_ref, vb_ref, ob_ref, stb_ref, True))

    def body(ci, carry):
        rows = (pl.ds(pl.multiple_of(ci * L, L), L),
                pl.ds(pl.multiple_of((nchunk - 1 - ci) * L, L), L))
        gates = [_hgrn_gates(f_ref[0, rows[di], :].astype(F32), lb_all[di:di + 1],
                             log2_one_m_lb[di:di + 1])
                 for di, (_, f_ref, _, _, _, _) in enumerate(dirs)]
        dec = [_hgrn_decays(q_ref[0, rows[di], :].astype(F32), gates[di][0], gates[di][1],
                            tri_ref[di], rev)
               for di, (q_ref, _, _, _, _, rev) in enumerate(dirs)]
        units = [(di, h) for h in range(HG_HEADS) for di in range(2)]
        scores = {}
        for di, h in units:
            cols = slice(h * HG_KEY, (h + 1) * HG_KEY)
            qq, kk = dec[di][0], dec[di][1]
            scores[di, h] = jnp.concatenate(
                [lax.dot_general(qq[i * HG_SUB:(i + 1) * HG_SUB, cols], kk[i][:, cols], _NT,
                                 preferred_element_type=F32) for i in range(nsub)], axis=0)
        for di, h in units:
            cols = slice(h * HG_KEY, (h + 1) * HG_KEY)
            _, _, v_ref, o_ref, st_ref, rev = dirs[di]
            _, _, qg, kd, e_end = dec[di]
            causal = (s_idx >= t_idx) if rev else (s_idx <= t_idx)
            a = jnp.where(causal, scores[di, h], 0.0).astype(BF16)
            vh = v_ref[0, rows[di], cols]
            st = st_ref[h]
            o = (jnp.dot(a, vh, preferred_element_type=F32)
                 + lax.dot_general(qg[:, cols], st.astype(BF16), _NT, preferred_element_type=F32))
            o_ref[0, rows[di], cols] = o.astype(BF16)
            st_ref[h] = (st * e_end[:, cols]
                         + lax.dot_general(vh, kd[:, cols], _TN, preferred_element_type=F32))
        return carry

    lax.fori_loop(0, nchunk, body, 0)


def _hgrn(z, lb_params, tri):
    b, s, _ = z.shape
    nb = s // HG_BLOCK

    def fwd(col):
        return pl.BlockSpec((1, HG_BLOCK, HG_WIDTH), lambda bi, j: (bi, j, col))

    def bwd(col):
        return pl.BlockSpec((1, HG_BLOCK, HG_WIDTH), lambda bi, j: (bi, nb - 1 - j, col))

    out = jax.ShapeDtypeStruct((b, s, HG_WIDTH), BF16)
    return pl.pallas_call(
        _hgrn_kernel,
        grid=(b, nb),
        in_specs=[fwd(COL_HQ), fwd(COL_HFF), fwd(COL_HI),
                  bwd(COL_HQ), bwd(COL_HFB), bwd(COL_HI),
                  pl.BlockSpec(lb_params.shape, lambda bi, j: (0, 0, 0)),
                  pl.BlockSpec((2, HG_CHUNK, HG_CHUNK), lambda bi, j: (0, 0, 0))],
        out_specs=[pl.BlockSpec((1, HG_BLOCK, HG_WIDTH), lambda bi, j: (bi, j, 0)),
                   pl.BlockSpec((1, HG_BLOCK, HG_WIDTH), lambda bi, j: (bi, nb - 1 - j, 0))],
        out_shape=[out, out],
        scratch_shapes=[pltpu.VMEM((HG_HEADS, HG_KEY, HG_KEY), F32),
                        pltpu.VMEM((HG_HEADS, HG_KEY, HG_KEY), F32)],
        compiler_params=_cparams(("arbitrary", "arbitrary")),
        name="hgrn",
    )(z, z, z, z, z, z, lb_params, tri)


ATT_PAIRS = ATT_HEADS // 2


def _att_geometry(d):
    nq = ATT_TILE // d
    assert nq % ATT_GQ == 0
    return nq, nq // ATT_GQ, nq + 2 * HALF_WINDOW


ATT_SLAB = 256
ATT_PERM_ROWS = 256


def _perm_matrices():
    n = ATT_PERM_ROWS
    mats = []
    for d in DILATIONS[1:]:
        p = np.zeros((n, n), np.float32)
        i = np.arange(n)
        p[i, d * (i % (n // d)) + i // (n // d)] = 1.0
        mats.append(p)
    return jnp.asarray(np.stack(mats), dtype=BF16)


ATT_REGROUP_BATCH = 8


def _attn_regroup(jobs, perm_ref, lanes):
    for i in range(0, len(jobs), ATT_REGROUP_BATCH):
        batch = jobs[i:i + ATT_REGROUP_BATCH]
        ys = [jnp.dot(perm_ref[p - 1], src_ref[0, row:row + ATT_PERM_ROWS, lanes],
                      preferred_element_type=F32)
              for p, d, src_ref, row, _, _, _, _ in batch]
        for (p, d, _, _, dst_ref, m0, rows_per_res, scale), y in zip(batch, ys):
            bm = ATT_PERM_ROWS // d
            if scale is not None:
                y = y * scale
            y = y.astype(BF16)
            for r in range(d):
                dst_ref[r * rows_per_res + m0:r * rows_per_res + m0 + bm, :] = y[r * bm:(r + 1) * bm]


def _attn_kernel(q_ref, kp_ref, kc_ref, kn_ref, vp_ref, vc_ref, vn_ref, bias_ref, perm_ref, o_ref,
                 q1_ref, k1_ref, v1_ref, q4_ref, k4_ref, v4_ref, q16_ref, k16_ref, v16_ref,
                 *part_refs, seq_len):
    T = ATT_TILE
    gq, gk = ATT_GQ, ATT_GQ + 2 * HALF_WINDOW
    t0 = pl.program_id(1) * T
    q_res = (q1_ref, q4_ref, q16_ref)
    k_res = (k1_ref, k4_ref, k16_ref)
    v_res = (v1_ref, v4_ref, v16_ref)
    parts = [part_refs[3 * p:3 * p + 3] for p in range(len(DILATIONS))]
    lane = lax.broadcasted_iota(jnp.int32, (1, 128), 1)
    first = lane < ATT_HEAD_DIM
    jk = lax.broadcasted_iota(jnp.int32, (1, gk), 1)
    q_scale = ATT_HEAD_DIM ** -0.5 * LOG2E
    kv_srcs = ((k_res, (kp_ref, kc_ref, kn_ref)), (v_res, (vp_ref, vc_ref, vn_ref)))

    def slab_body(slab, carry):
        sl = pl.ds(pl.multiple_of(slab * ATT_SLAB, ATT_SLAB), ATT_SLAB)
        q1_ref[...] = (q_ref[0, :, sl].astype(F32) * q_scale).astype(BF16)
        for res, (prv, cur, nxt) in kv_srcs:
            res[0][0:HALF_WINDOW, :] = prv[0, ATT_HALO - HALF_WINDOW:ATT_HALO, sl]
            res[0][HALF_WINDOW:HALF_WINDOW + T, :] = cur[0, :, sl]
            res[0][HALF_WINDOW + T:, :] = nxt[0, 0:HALF_WINDOW, sl]
        jobs = []
        for p, d in enumerate(DILATIONS):
            if p == 0:
                continue
            nq, _, nk = _att_geometry(d)
            halo = d * HALF_WINDOW
            bm = ATT_PERM_ROWS // d

            def blocks(src, row0, ntok, dst, m0, rows_per_res, scale, p=p, d=d, bm=bm):
                return [(p, d, src, row0 + b * ATT_PERM_ROWS, dst, m0 + b * bm, rows_per_res, scale)
                        for b in range(ntok // ATT_PERM_ROWS)]

            jobs += blocks(q_ref, 0, T, q_res[p], 0, nq, q_scale)
            for res, (prv, cur, nxt) in kv_srcs:
                jobs += blocks(prv, ATT_HALO - halo, halo, res[p], 0, nk, None)
                jobs += blocks(cur, 0, T, res[p], HALF_WINDOW, nk, None)
                jobs += blocks(nxt, 0, halo, res[p], HALF_WINDOW + nq, nk, None)
        _attn_regroup(jobs, perm_ref, sl)

        def pair_body(j, carry):
            pair = slab * (ATT_SLAB // 128) + j
            lanes = pl.ds(pl.multiple_of(j * 128, 128), 128)
            for p, d in enumerate(DILATIONS):
                _attn_pattern(p, d, pair, lanes, t0, seq_len, first, jk, q_res[p], k_res[p], v_res[p],
                              bias_ref, parts[p])
            for p in range(len(DILATIONS) - 1, 0, -1):
                _attn_merge(parts[p - 1], parts[p], DILATIONS[p - 1], DILATIONS[p])
            o_ref[0, :, pl.ds(pl.multiple_of(pair * 128, 128), 128)] = \
                (parts[0][0][...] / parts[0][2][...]).astype(BF16)
            return carry

        lax.fori_loop(0, ATT_SLAB // 128, pair_body, 0)
        return carry

    lax.fori_loop(0, ATT_WIDTH // ATT_SLAB, slab_body, 0)


def _attn_merge(run, pat, d_run, d_pat):
    accn_ref, mn_ref, ln_ref = run
    accp_ref, mp_ref, lp_ref = pat
    nq = ATT_TILE // d_pat
    for r in range(d_pat):
        nat = (pl.ds((r % d_run) * (ATT_TILE // d_run) + r // d_run, nq, stride=d_pat // d_run),
               slice(None))
        res = (slice(r * nq, (r + 1) * nq), slice(None))
        m_old = mn_ref[nat]
        m_pat = mp_ref[res]
        m_new = jnp.maximum(m_old, m_pat)
        a_old = jnp.exp2(m_old - m_new)
        a_pat = jnp.exp2(m_pat - m_new)
        accn_ref[nat] = accn_ref[nat] * a_old + accp_ref[res] * a_pat
        ln_ref[nat] = ln_ref[nat] * a_old + lp_ref[res] * a_pat
        mn_ref[nat] = m_new


def _attn_pattern(p, d, pair, lanes, t0, seq_len, first, jk, q_ref, k_ref, v_ref, bias_ref, dst):
    nq, gpr, nk = _att_geometry(d)
    gq, gk = ATT_GQ, ATT_GQ + 2 * HALF_WINDOW
    gpt = ATT_TRIP_QUERIES // gq

    def trip_body(ti, carry):
        b2 = bias_ref[p, pair]
        q0s, k0s, kvalids = [], [], []
        for sub in range(gpt):
            gi = ti * gpt + sub
            r = gi // gpr
            g = gi % gpr
            q0s.append(pl.multiple_of(gi * gq, gq))
            k0s.append(pl.multiple_of(r * nk + g * gq, gq))
            ktok = t0 + r + d * (g * gq - HALF_WINDOW + jk)
            kvalids.append((ktok >= 0) & (ktok < seq_len))
        scs = []
        for sub in range(gpt):
            qg = q_ref[pl.ds(q0s[sub], gq), lanes]
            kg = k_ref[pl.ds(k0s[sub], gk), lanes]
            zero = jnp.zeros_like(qg)
            q2 = jnp.concatenate([jnp.where(first, qg, zero), jnp.where(first, zero, qg)], axis=0)
            scs.append(lax.dot_general(q2, kg, _NT, preferred_element_type=F32))
        pes, mxs = [], []
        for sub in range(gpt):
            sc = jnp.where(kvalids[sub], scs[sub] + b2, NEG)
            mx = jnp.max(sc, axis=-1, keepdims=True)
            pes.append(jnp.exp2(sc - mx))
            mxs.append(mx)
        for sub in range(gpt):
            pe, mx = pes[sub], mxs[sub]
            ls = jnp.sum(pe, axis=-1, keepdims=True)
            vg = v_ref[pl.ds(k0s[sub], gk), lanes]
            pv = jnp.dot(pe.astype(BF16), vg, preferred_element_type=F32)
            rows = (pl.ds(q0s[sub], gq), slice(None))
            dst[0][rows] = jnp.where(first, pv[0:gq], pv[gq:2 * gq])
            dst[1][rows] = jnp.where(first, mx[0:gq], mx[gq:2 * gq])
            dst[2][rows] = jnp.where(first, ls[0:gq], ls[gq:2 * gq])
        return carry

    lax.fori_loop(0, ATT_TILE // ATT_TRIP_QUERIES, trip_body, 0)


def _attn(z, bias):
    b, s, _ = z.shape
    perm = _perm_matrices()
    nt = s // ATT_TILE
    blk = (1, ATT_TILE, ATT_WIDTH)
    halo_blk = (1, ATT_HALO, ATT_WIDTH)
    hpt = ATT_TILE // ATT_HALO
    nh = s // ATT_HALO

    def cur(col):
        return pl.BlockSpec(blk, lambda bi, i: (bi, i, col))

    def prev(col):
        return pl.BlockSpec(halo_blk, lambda bi, i: (bi, jnp.maximum(i * hpt - 1, 0), col))

    def nxt(col):
        return pl.BlockSpec(halo_blk, lambda bi, i: (bi, jnp.minimum((i + 1) * hpt, nh - 1), col))

    return pl.pallas_call(
        functools.partial(_attn_kernel, seq_len=s),
        grid=(b, nt),
        in_specs=[cur(COL_AQ), prev(COL_AK), cur(COL_AK), nxt(COL_AK),
                  prev(COL_AV), cur(COL_AV), nxt(COL_AV),
                  pl.BlockSpec(bias.shape, lambda bi, i: (0, 0, 0, 0), pipeline_mode=pl.Buffered(1)),
                  pl.BlockSpec(perm.shape, lambda bi, i: (0, 0, 0), pipeline_mode=pl.Buffered(1))],
        out_specs=pl.BlockSpec(blk, lambda bi, i: (bi, i, 0)),
        out_shape=jax.ShapeDtypeStruct((b, s, ATT_WIDTH), BF16),
        scratch_shapes=(
            [pltpu.VMEM((rows, ATT_SLAB), BF16)
             for d in DILATIONS
             for rows in (ATT_TILE, d * _att_geometry(d)[2], d * _att_geometry(d)[2])]
            + [pltpu.VMEM((ATT_TILE, 128), F32) for _ in range(3 * len(DILATIONS))]),
        compiler_params=_cparams(("arbitrary", "arbitrary")),
        name="attn",
    )(z, z, z, z, z, z, z, bias, perm)


def _silu(t):
    return t * (0.5 + 0.5 * jnp.tanh(0.5 * t))


def _out_kernel(x_ref, of_ref, ob_ref, g_ref, att_ref, mod_ref, gnw_ref, anw_ref, n2w_ref, fnw_ref,
                wo_ref, wg_ref, wu_ref, wd_ref, out_ref, h2_ref, acc_ref, act_ref):
    x = x_ref[0]
    o = of_ref[0].astype(F32) + ob_ref[0].astype(F32)
    gate = _silu(g_ref[0].astype(F32))
    parts = []
    for h in range(HG_HEADS):
        cols = slice(h * HG_KEY, (h + 1) * HG_KEY)
        oh = o[:, cols]
        ms = jnp.mean(oh * oh, axis=-1, keepdims=True)
        parts.append(oh * lax.rsqrt(ms + EPS) * gnw_ref[:, cols] * gate[:, cols])
    att = att_ref[0].astype(F32)
    ms = jnp.mean(att * att, axis=-1, keepdims=True)
    parts.append(att * lax.rsqrt(ms + EPS) * anw_ref[...])
    y = jnp.concatenate(parts, axis=-1).astype(BF16)
    mix = jnp.dot(y, wo_ref[...], preferred_element_type=F32)
    x1 = x + mod_ref[0, 2:3, :] * mix

    ms = jnp.mean(x1 * x1, axis=-1, keepdims=True)
    h2 = x1 * lax.rsqrt(ms + EPS) * n2w_ref[...]
    h2_ref[...] = (h2 * (1.0 + mod_ref[0, 4:5, :]) + mod_ref[0, 3:4, :]).astype(BF16)

    nchunks = D_FF // FF_CHUNK

    def act_chunk(j):
        c0 = pl.multiple_of(j * FF_CHUNK, FF_CHUNK)
        h2b = h2_ref[...]
        gt = jnp.dot(h2b, wg_ref[:, pl.ds(c0, FF_CHUNK)], preferred_element_type=F32)
        up = jnp.dot(h2b, wu_ref[:, pl.ds(c0, FF_CHUNK)], preferred_element_type=F32)
        return gt, up

    def down(j):
        c0 = pl.multiple_of(j * FF_CHUNK, FF_CHUNK)
        return jnp.dot(act_ref[j % 2], wd_ref[pl.ds(c0, FF_CHUNK), :], preferred_element_type=F32)

    gt, up = act_chunk(0)
    act_ref[0] = (_silu(gt) * up).astype(BF16)
    acc_ref[...] = jnp.zeros_like(acc_ref)

    def ff_body(j, carry):
        gt, up = act_chunk(j + 1)
        acc_ref[...] += down(j)
        act_ref[(j + 1) % 2] = (_silu(gt) * up).astype(BF16)
        return carry

    lax.fori_loop(0, nchunks - 1, ff_body, 0, unroll=True)
    x2 = x1 + mod_ref[0, 5:6, :] * (acc_ref[...] + down(nchunks - 1))
    ms = jnp.mean(x2 * x2, axis=-1, keepdims=True)
    out_ref[0] = x2 * lax.rsqrt(ms + EPS) * fnw_ref[...]


def _out(x, o_f, o_b, z, att, mod, gn_w, an_w, n2_w, fn_w, wo, wg, wu, wd):
    b, s, d = x.shape

    def tok(width, col=0):
        return pl.BlockSpec((1, TM_OUT, width), lambda bi, i: (bi, i, col))

    def const(shape):
        return pl.BlockSpec(shape, lambda bi, i: (0,) * len(shape), pipeline_mode=pl.Buffered(1))

    return pl.pallas_call(
        _out_kernel,
        grid=(b, s // TM_OUT),
        in_specs=[tok(d), tok(HG_WIDTH), tok(HG_WIDTH), tok(HG_WIDTH, COL_HG), tok(ATT_WIDTH),
                  pl.BlockSpec((1, 6, d), lambda bi, i: (bi, 0, 0)),
                  const((1, HG_WIDTH)), const((1, ATT_WIDTH)), const((1, d)), const((1, d)),
                  const(wo.shape), const(wg.shape), const(wu.shape), const(wd.shape)],
        out_specs=tok(d),
        out_shape=jax.ShapeDtypeStruct((b, s, d), F32),
        scratch_shapes=[pltpu.VMEM((TM_OUT, d), BF16), pltpu.VMEM((TM_OUT, d), F32),
                        pltpu.VMEM((2, TM_OUT, FF_CHUNK), BF16)],
        compiler_params=_cparams(("arbitrary", "arbitrary")),
        name="out_ffn",
    )(x, o_f, o_b, z, att, mod, gn_w, an_w, n2_w, fn_w, wo, wg, wu, wd)


def _cumsum_matrices():
    t = np.arange(HG_CHUNK)
    fwd = (t[None, :] <= t[:, None]).astype(np.float32)
    bwd = (t[None, :] >= t[:, None]).astype(np.float32)
    return jnp.asarray(np.stack([fwd, bwd]), dtype=BF16)


def kernel(x, c, rel_bias, w_ada, b_ada, norm1_w, w_in, hg_lower_bound, hg_norm_w, attn_norm_w,
           w_out, norm2_w, w_gate, w_up, w_down, final_norm_w):
    b, s, d = x.shape
    assert d == D_MODEL and s % ATT_TILE == 0 and s % HG_BLOCK == 0 and w_ada.shape[0] == 1
    c_pad = jnp.zeros((8, d), F32).at[:b].set(c)
    mod = _ada(c_pad, w_ada[0], b_ada[0][None, :])[:b].reshape(b, 6, d)

    bias = _bias_tiles(rel_bias.astype(F32), jnp.asarray(_bucket_tiles(ATT_GQ)))
    bias = bias.reshape(len(DILATIONS), ATT_PAIRS, 2 * ATT_GQ, ATT_GQ + 2 * HALF_WINDOW)
    z = _inproj(x, mod, norm1_w[0][None, :], w_in[0].astype(BF16))
    o_f, o_b = _hgrn(z, hg_lower_bound.astype(F32), _cumsum_matrices())
    att = _attn(z, bias)
    return _out(x, o_f, o_b, z, att, mod,
                hg_norm_w[0][None, :], attn_norm_w[0][None, :], norm2_w[0][None, :],
                final_norm_w[None, :],
                w_out[0].astype(BF16), w_gate[0].astype(BF16), w_up[0].astype(BF16),
                w_down[0].astype(BF16))
```

```python
import functools
import math

import numpy as np
import jax
import jax.numpy as jnp
from jax import lax
from jax.experimental import pallas as pl
from jax.experimental.pallas import tpu as pltpu

F32 = jnp.float32
BF16 = jnp.bfloat16

D_MODEL = 1024
HG_HEADS = 4
HG_KEY = 128
HG_WIDTH = HG_HEADS * HG_KEY
ATT_HEADS = 8
ATT_HEAD_DIM = 64
ATT_WIDTH = ATT_HEADS * ATT_HEAD_DIM
DILATIONS = (1, 4, 16)
HALF_WINDOW = 64
NUM_BUCKETS = 32
MAX_DISTANCE = 1024
D_FF = 2816
EPS = 1e-6
IN_WIDTH = 5 * HG_WIDTH + 3 * ATT_WIDTH
COL_HQ, COL_HFF, COL_HFB, COL_HI, COL_HG, COL_AQ, COL_AK, COL_AV = range(8)

VMEM_LIMIT_BYTES = 56 * 1024 * 1024

TM_IN = 512
TM_OUT = 512
FF_CHUNK = 256
HG_CHUNK = 128
HG_SUB = 64
HG_BLOCK = 512
ATT_TILE = 2048
ATT_HALO = HALF_WINDOW * max(DILATIONS)
ATT_GQ = 128
ATT_TRIP_QUERIES = 1024
NEG = -1e30
LOG2E = 1.4426950408889634


def _cparams(sem):
    return pltpu.CompilerParams(dimension_semantics=sem, vmem_limit_bytes=VMEM_LIMIT_BYTES)


def _ada_kernel(c_ref, w_ref, b_ref, o_ref):
    c = c_ref[...]
    sc = c * (1.0 / (1.0 + jnp.exp(-c)))
    o_ref[...] = jnp.dot(sc, w_ref[...], preferred_element_type=F32,
                         precision=lax.Precision.HIGHEST) + b_ref[...]


def _ada(c_pad, w_ada, b_ada):
    rows, d = c_pad.shape
    n = w_ada.shape[1]
    bn = 1536
    return pl.pallas_call(
        _ada_kernel,
        grid=(n // bn,),
        in_specs=[pl.BlockSpec((rows, d), lambda j: (0, 0)),
                  pl.BlockSpec((d, bn), lambda j: (0, j)),
                  pl.BlockSpec((1, bn), lambda j: (0, j))],
        out_specs=pl.BlockSpec((rows, bn), lambda j: (0, j)),
        out_shape=jax.ShapeDtypeStruct((rows, n), F32),
        compiler_params=_cparams(("arbitrary",)),
        name="ada",
    )(c_pad, w_ada, b_ada)


def _t5_bucket_np(rel):
    half = NUM_BUCKETS // 2
    max_exact = half // 2
    ret = np.where(rel > 0, half, 0)
    n = np.abs(rel)
    nf = np.maximum(n, 1).astype(np.float64)
    large = max_exact + (np.log(nf / max_exact) / math.log(MAX_DISTANCE / max_exact)
                         * (half - max_exact)).astype(np.int64)
    large = np.minimum(large, half - 1)
    return (ret + np.where(n < max_exact, n, large)).astype(np.int32)


def _bucket_tiles(gq):
    gk = gq + 2 * HALF_WINDOW
    off = (np.arange(gk)[None, :] - HALF_WINDOW) - np.arange(gq)[:, None]
    tiles = []
    for d in DILATIONS:
        tiles.append(np.where(np.abs(off) <= HALF_WINDOW, _t5_bucket_np(off * d), -1))
    return np.stack(tiles).astype(np.int32)


def _bias_kernel(rb_ref, bk_ref, o_ref):
    bk = bk_ref[0]
    for h in range(ATT_HEADS):
        acc = jnp.full(bk.shape, NEG, F32)
        for b in range(NUM_BUCKETS):
            acc = jnp.where(bk == b, rb_ref[b, h] * LOG2E, acc)
        o_ref[0, h] = acc


def _bias_tiles(rel_bias, buckets):
    npat, gq, gk = buckets.shape
    return pl.pallas_call(
        _bias_kernel,
        grid=(npat,),
        in_specs=[pl.BlockSpec(memory_space=pltpu.SMEM),
                  pl.BlockSpec((1, gq, gk), lambda p: (p, 0, 0))],
        out_specs=pl.BlockSpec((1, ATT_HEADS, gq, gk), lambda p: (p, 0, 0, 0)),
        out_shape=jax.ShapeDtypeStruct((npat, ATT_HEADS, gq, gk), F32),
        compiler_params=_cparams(("arbitrary",)),
        name="bias_tiles",
    )(rel_bias, buckets)


def _inproj_kernel(x_ref, mod_ref, nw_ref, w_ref, z_ref):
    x = x_ref[0]
    ms = jnp.mean(x * x, axis=-1, keepdims=True)
    y = x * lax.rsqrt(ms + EPS) * nw_ref[...]
    shift = mod_ref[0, 0:1, :]
    scale = mod_ref[0, 1:2, :]
    h = (y * (1.0 + scale) + shift).astype(BF16)
    for j in range(IN_WIDTH // 512):
        cols = slice(j * 512, (j + 1) * 512)
        z_ref[0, :, cols] = jnp.dot(h, w_ref[:, cols], preferred_element_type=F32).astype(BF16)


def _inproj(x, mod, norm_w, w_in_bf16):
    b, s, d = x.shape
    return pl.pallas_call(
        _inproj_kernel,
        grid=(b, s // TM_IN),
        in_specs=[pl.BlockSpec((1, TM_IN, d), lambda bi, i: (bi, i, 0)),
                  pl.BlockSpec((1, 6, d), lambda bi, i: (bi, 0, 0)),
                  pl.BlockSpec((1, d), lambda bi, i: (0, 0)),
                  pl.BlockSpec((d, IN_WIDTH), lambda bi, i: (0, 0), pipeline_mode=pl.Buffered(1))],
        out_specs=pl.BlockSpec((1, TM_IN, IN_WIDTH), lambda bi, i: (bi, i, 0)),
        out_shape=jax.ShapeDtypeStruct((b, s, IN_WIDTH), BF16),
        compiler_params=_cparams(("arbitrary", "arbitrary")),
        name="inproj",
    )(x, mod, norm_w, w_in_bf16)


_NT =(((1,), (1,)), ((), ()))
_TN = (((0,), (0,)), ((), ()))


def _hgrn_gates(z, lb, log2_one_m_lb):
    e = jnp.exp(-jnp.abs(z))
    ope = 1.0 + e
    l1 = jnp.log2(ope)
    w = jnp.where(z >= 0, 1.0, e)
    logf = jnp.log2(lb * ope + (1.0 - lb) * w) - l1
    logk = log2_one_m_lb - jnp.maximum(z, 0.0) * LOG2E - l1
    return logf, logk


def _hgrn_decays(q, logf, logk, tri, reverse):
    L = HG_CHUNK
    hi = logf.astype(BF16)
    lo = (logf - hi.astype(F32)).astype(BF16)
    res = jnp.dot(tri, jnp.concatenate([hi, lo], axis=1), preferred_element_type=F32)
    g = res[:, :HG_WIDTH] + res[:, HG_WIDTH:]

    nsub = L // HG_SUB
    ref_off = HG_SUB // 2 if reverse else HG_SUB // 2 - 1
    grefs = [g[i * HG_SUB + ref_off:i * HG_SUB + ref_off + 1, :] for i in range(nsub)]
    gref_full = jnp.concatenate([jnp.broadcast_to(r, (HG_SUB, HG_WIDTH)) for r in grefs], axis=0)
    egref_full = jnp.concatenate([jnp.broadcast_to(jnp.exp2(r), (HG_SUB, HG_WIDTH)) for r in grefs],
                                 axis=0)
    qq32 = q * jnp.exp2(g - gref_full)
    qq = qq32.astype(BF16)
    qg = (qq32 * egref_full).astype(BF16)

    lkg = logk - g
    g_end = g[0:1, :] if reverse else g[L - 1:L, :]
    kk = []
    for i in range(nsub):
        lo_r, hi_r = (i * HG_SUB, L) if reverse else (0, (i + 1) * HG_SUB)
        part = jnp.exp2(lkg[lo_r:hi_r] + grefs[i])
        if hi_r - lo_r == L:
            kd = (part * jnp.exp2(g_end - grefs[i])).astype(BF16)
        pad = jnp.zeros((L - (hi_r - lo_r), HG_WIDTH), BF16)
        pieces = [pad, part.astype(BF16)] if reverse else [part.astype(BF16), pad]
        kk.append(jnp.concatenate(pieces, axis=0) if pad.shape[0] else part.astype(BF16))
    return qq, kk, qg, kd, jnp.exp2(g_end)


def _hgrn_kernel(qf_ref, ff_ref, vf_ref, qb_ref, fb_ref, vb_ref, lbp_ref, tri_ref,
                 of_ref, ob_ref, stf_ref, stb_ref):
    @pl.when(pl.program_id(1) == 0)
    def _():
        stf_ref[...] = jnp.zeros_like(stf_ref)
        stb_ref[...] = jnp.zeros_like(stb_ref)

    lbp = lbp_ref[...]
    lbe = jnp.exp(lbp - jnp.max(lbp, axis=0, keepdims=True))
    lb_all = lbe[0] / jnp.sum(lbe, axis=0)
    log2_one_m_lb = jnp.log2(1.0 - lb_all)

    L = HG_CHUNK
    nchunk = HG_BLOCK // L
    nsub = L // HG_SUB
    t_idx = lax.broadcasted_iota(jnp.int32, (L, L), 0)
    s_idx = lax.broadcasted_iota(jnp.int32, (L, L), 1)
    dirs = ((qf_ref, ff_ref, vf_ref, of_ref, stf_ref, False),
            (qb_ref, fb_ref, vb_ref, ob_ref, stb_ref, True))

    units = [(di, h) for h in range(HG_HEADS) for di in range(2)]

    def chunk_rows(ci):
        return (pl.ds(pl.multiple_of(ci * L, L), L),
                pl.ds(pl.multiple_of((nchunk - 1 - ci) * L, L), L))

    def body(ci, carry):
        rows = chunk_rows(ci)
        gates = [_hgrn_gates(f_ref[0, rows[di], :].astype(F32), lb_all[di:di + 1],
                             log2_one_m_lb[di:di + 1])
                 for di, (_, f_ref, _, _, _, _) in enumerate(dirs)]
        dec = [_hgrn_decays(q_ref[0, rows[di], :].astype(F32), gates[di][0], gates[di][1],
                            tri_ref[di], rev)
               for di, (q_ref, _, _, _, _, rev) in enumerate(dirs)]
        scores = {}
        for di, h in units:
            cols = slice(h * HG_KEY, (h + 1) * HG_KEY)
            qq, kk = dec[di][0], dec[di][1]
            scores[di, h] = jnp.concatenate(
                [lax.dot_general(qq[i * HG_SUB:(i + 1) * HG_SUB, cols], kk[i][:, cols], _NT,
                                 preferred_element_type=F32) for i in range(nsub)], axis=0)
        for di, h in units:
            cols = slice(h * HG_KEY, (h + 1) * HG_KEY)
            _, _, v_ref, o_ref, st_ref, rev = dirs[di]
            _, _, qg, kd, e_end = dec[di]
            causal = (s_idx >= t_idx) if rev else (s_idx <= t_idx)
            a = jnp.where(causal, scores[di, h], 0.0).astype(BF16)
            vh = v_ref[0, rows[di], cols]
            st = st_ref[h]
            o = (jnp.dot(a, vh, preferred_element_type=F32)
                 + lax.dot_general(qg[:, cols], st.astype(BF16), _NT, preferred_element_type=F32))
            o_ref[0, rows[di], cols] = o.astype(BF16)
            st_ref[h] = (st * e_end[:, cols]
                         + lax.dot_general(vh, kd[:, cols], _TN, preferred_element_type=F32))
        return carry

    lax.fori_loop(0, nchunk, body, 0)


def _hgrn(z, lb_params, tri):
    b, s, _ = z.shape
    nb = s // HG_BLOCK

    def fwd(col):
        return pl.BlockSpec((1, HG_BLOCK, HG_WIDTH), lambda bi, j: (bi, j, col))

    def bwd(col):
        return pl.BlockSpec((1, HG_BLOCK, HG_WIDTH), lambda bi, j: (bi, nb - 1 - j, col))

    out = jax.ShapeDtypeStruct((b, s, HG_WIDTH), BF16)
    return pl.pallas_call(
        _hgrn_kernel,
        grid=(b, nb),
        in_specs=[fwd(COL_HQ), fwd(COL_HFF), fwd(COL_HI),
                  bwd(COL_HQ), bwd(COL_HFB), bwd(COL_HI),
                  pl.BlockSpec(lb_params.shape, lambda bi, j: (0, 0, 0)),
                  pl.BlockSpec((2, HG_CHUNK, HG_CHUNK), lambda bi, j: (0, 0, 0))],
        out_specs=[pl.BlockSpec((1, HG_BLOCK, HG_WIDTH), lambda bi, j: (bi, j, 0)),
                   pl.BlockSpec((1, HG_BLOCK, HG_WIDTH), lambda bi, j: (bi, nb - 1 - j, 0))],
        out_shape=[out, out],
        scratch_shapes=[pltpu.VMEM((HG_HEADS, HG_KEY, HG_KEY), F32),
                        pltpu.VMEM((HG_HEADS, HG_KEY, HG_KEY), F32)],
        compiler_params=_cparams(("arbitrary", "arbitrary")),
        name="hgrn",
    )(z, z, z, z, z, z, lb_params, tri)


ATT_PAIRS = ATT_HEADS // 2


def _att_geometry(d):
    nq = ATT_TILE // d
    assert nq % ATT_GQ == 0
    return nq, nq // ATT_GQ, nq + 2 * HALF_WINDOW


ATT_SLAB = 256
ATT_PERM_ROWS = 256


def _perm_matrices():
    n = ATT_PERM_ROWS
    mats = []
    for d in DILATIONS[1:]:
        p = np.zeros((n, n), np.float32)
        i = np.arange(n)
        p[i, d * (i % (n // d)) + i // (n // d)] = 1.0
        mats.append(p)
    return jnp.asarray(np.stack(mats), dtype=BF16)


ATT_REGROUP_BATCH = 8


def _attn_regroup(jobs, perm_ref, lanes):
    for i in range(0, len(jobs), ATT_REGROUP_BATCH):
        batch = jobs[i:i + ATT_REGROUP_BATCH]
        ys = [jnp.dot(perm_ref[p - 1], src_ref[0, row:row + ATT_PERM_ROWS, lanes],
                      preferred_element_type=F32)
              for p, d, src_ref, row, _, _, _, _ in batch]
        for (p, d, _, _, dst_ref, m0, rows_per_res, scale), y in zip(batch, ys):
            bm = ATT_PERM_ROWS // d
            if scale is not None:
                y = y * scale
            y = y.astype(BF16)
            for r in range(d):
                dst_ref[r * rows_per_res + m0:r * rows_per_res + m0 + bm, :] = y[r * bm:(r + 1) * bm]


def _attn_kernel(q_ref, kp_ref, kc_ref, kn_ref, vp_ref, vc_ref, vn_ref, bias_ref, perm_ref, o_ref,
                 q1_ref, k1_ref, v1_ref, q4_ref, k4_ref, v4_ref, q16_ref, k16_ref, v16_ref,
                 sc_ref, *part_refs, seq_len):
    T = ATT_TILE
    gq, gk = ATT_GQ, ATT_GQ + 2 * HALF_WINDOW
    t0 = pl.program_id(1) * T
    q_res = (q1_ref, q4_ref, q16_ref)
    k_res = (k1_ref, k4_ref, k16_ref)
    v_res = (v1_ref, v4_ref, v16_ref)
    parts = [part_refs[3 * p:3 * p + 3] for p in range(len(DILATIONS))]
    lane = lax.broadcasted_iota(jnp.int32, (1, 128), 1)
    first = lane < ATT_HEAD_DIM
    jk = lax.broadcasted_iota(jnp.int32, (1, gk), 1)
    q_scale = ATT_HEAD_DIM ** -0.5 * LOG2E
    kv_srcs = ((k_res, (kp_ref, kc_ref, kn_ref)), (v_res, (vp_ref, vc_ref, vn_ref)))

    def slab_body(slab, carry):
        sl = pl.ds(pl.multiple_of(slab * ATT_SLAB, ATT_SLAB), ATT_SLAB)
        q1_ref[...] = (q_ref[0, :, sl].astype(F32) * q_scale).astype(BF16)
        for res, (prv, cur, nxt) in kv_srcs:
            res[0][0:HALF_WINDOW, :] = prv[0, ATT_HALO - HALF_WINDOW:ATT_HALO, sl]
            res[0][HALF_WINDOW:HALF_WINDOW + T, :] = cur[0, :, sl]
            res[0][HALF_WINDOW + T:, :] = nxt[0, 0:HALF_WINDOW, sl]
        jobs = []
        for p, d in enumerate(DILATIONS):
            if p == 0:
                continue
            nq, _, nk = _att_geometry(d)
            halo = d * HALF_WINDOW
            bm = ATT_PERM_ROWS // d

            def blocks(src, row0, ntok, dst, m0, rows_per_res, scale, p=p, d=d, bm=bm):
                return [(p, d, src, row0 + b * ATT_PERM_ROWS, dst, m0 + b * bm, rows_per_res, scale)
                        for b in range(ntok // ATT_PERM_ROWS)]

            jobs += blocks(q_ref, 0, T, q_res[p], 0, nq, q_scale)
            for res, (prv, cur, nxt) in kv_srcs:
                jobs += blocks(prv, ATT_HALO - halo, halo, res[p], 0, nk, None)
                jobs += blocks(cur, 0, T, res[p], HALF_WINDOW, nk, None)
                jobs += blocks(nxt, 0, halo, res[p], HALF_WINDOW + nq, nk, None)
        _attn_regroup(jobs, perm_ref, sl)

        def pair_body(j, carry):
            pair = slab * (ATT_SLAB // 128) + j
            lanes = pl.ds(pl.multiple_of(j * 128, 128), 128)
            trips = [_AttnTrip(p, d, ti, pair, lanes, t0, seq_len, first, jk, q_res[p], k_res[p],
                               v_res[p], bias_ref, parts[p])
                     for p, d in enumerate(DILATIONS) for ti in range(T // ATT_TRIP_QUERIES)]
            for sub in range(trips[0].n):
                sc_ref[0, sub] = trips[0].scores(sub)
            for k, trip in enumerate(trips):
                slot = k % 2
                soft = []
                for sub in range(trip.n):
                    if k + 1 < len(trips):
                        sc_ref[1 - slot, sub] = trips[k + 1].scores(sub)
                    soft.append(trip.softmax(sub, sc_ref[slot, sub]))
                for sub in range(trip.n):
                    trip.output(sub, *soft[sub])
            for p in range(len(DILATIONS) - 1, 0, -1):
                _attn_merge(parts[p - 1], parts[p], DILATIONS[p - 1], DILATIONS[p])
            o_ref[0, :, pl.ds(pl.multiple_of(pair * 128, 128), 128)] = \
                (parts[0][0][...] / parts[0][2][...]).astype(BF16)
            return carry

        lax.fori_loop(0, ATT_SLAB // 128, pair_body, 0)
        return carry

    lax.fori_loop(0, ATT_WIDTH // ATT_SLAB, slab_body, 0)


def _attn_merge(run, pat, d_run, d_pat):
    accn_ref, mn_ref, ln_ref = run
    accp_ref, mp_ref, lp_ref = pat
    nq = ATT_TILE // d_pat
    for r in range(d_pat):
        nat = (pl.ds((r % d_run) * (ATT_TILE // d_run) + r // d_run, nq, stride=d_pat // d_run),
               slice(None))
        res = (slice(r * nq, (r + 1) * nq), slice(None))
        m_old = mn_ref[nat]
        m_pat = mp_ref[res]
        m_new = jnp.maximum(m_old, m_pat)
        a_old = jnp.exp2(m_old - m_new)
        a_pat = jnp.exp2(m_pat - m_new)
        accn_ref[nat] = accn_ref[nat] * a_old + accp_ref[res] * a_pat
        ln_ref[nat] = ln_ref[nat] * a_old + lp_ref[res] * a_pat
        mn_ref[nat] = m_new


class _AttnTrip:
    def __init__(self, p, d, ti, pair, lanes, t0, seq_len, first, jk, q_ref, k_ref, v_ref, bias_ref,
                 dst):
        nq, gpr, nk = _att_geometry(d)
        self.gq, self.gk = ATT_GQ, ATT_GQ + 2 * HALF_WINDOW
        self.n = ATT_TRIP_QUERIES // ATT_GQ
        self.p, self.pair, self.lanes, self.first = p, pair, lanes, first
        self.q_ref, self.k_ref, self.v_ref, self.bias_ref, self.dst = q_ref, k_ref, v_ref, bias_ref, dst
        self.q0, self.k0, self.kvalid = [], [], []
        for sub in range(self.n):
            gi = ti * self.n + sub
            r, g = gi // gpr, gi % gpr
            self.q0.append(gi * ATT_GQ)
            self.k0.append(r * nk + g * ATT_GQ)
            ktok = t0 + r + d * (g * ATT_GQ - HALF_WINDOW + jk)
            self.kvalid.append((ktok >= 0) & (ktok < seq_len))

    def scores(self, sub):
        qg = self.q_ref[self.q0[sub]:self.q0[sub] + self.gq, self.lanes]
        kg = self.k_ref[self.k0[sub]:self.k0[sub] + self.gk, self.lanes]
        zero = jnp.zeros_like(qg)
        q2 = jnp.concatenate([jnp.where(self.first, qg, zero), jnp.where(self.first, zero, qg)],
                             axis=0)
        return lax.dot_general(q2, kg, _NT, preferred_element_type=F32)

    def softmax(self, sub, sc):
        b2 = self.bias_ref[self.p, self.pair]
        sc = jnp.where(self.kvalid[sub], sc + b2, NEG)
        mx = jnp.max(sc, axis=-1, keepdims=True)
        return jnp.exp2(sc - mx), mx

    def output(self, sub, pe, mx):
        gq = self.gq
        ls = jnp.sum(pe, axis=-1, keepdims=True)
        vg = self.v_ref[self.k0[sub]:self.k0[sub] + self.gk, self.lanes]
        pv = jnp.dot(pe.astype(BF16), vg, preferred_element_type=F32)
        rows = (slice(self.q0[sub], self.q0[sub] + gq), slice(None))
        self.dst[0][rows] = jnp.where(self.first, pv[0:gq], pv[gq:2 * gq])
        self.dst[1][rows] = jnp.where(self.first, mx[0:gq], mx[gq:2 * gq])
        self.dst[2][rows] = jnp.where(self.first, ls[0:gq], ls[gq:2 * gq])


def _attn(z, bias):
    b, s, _ = z.shape
    perm = _perm_matrices()
    nt = s // ATT_TILE
    blk = (1, ATT_TILE, ATT_WIDTH)
    halo_blk = (1, ATT_HALO, ATT_WIDTH)
    hpt = ATT_TILE // ATT_HALO
    nh = s // ATT_HALO

    def cur(col):
        return pl.BlockSpec(blk, lambda bi, i: (bi, i, col))

    def prev(col):
        return pl.BlockSpec(halo_blk, lambda bi, i: (bi, jnp.maximum(i * hpt - 1, 0), col))

    def nxt(col):
        return pl.BlockSpec(halo_blk, lambda bi, i: (bi, jnp.minimum((i + 1) * hpt, nh - 1), col))

    return pl.pallas_call(
        functools.partial(_attn_kernel, seq_len=s),
        grid=(b, nt),
        in_specs=[cur(COL_AQ), prev(COL_AK), cur(COL_AK), nxt(COL_AK),
                  prev(COL_AV), cur(COL_AV), nxt(COL_AV),
                  pl.BlockSpec(bias.shape, lambda bi, i: (0, 0, 0, 0), pipeline_mode=pl.Buffered(1)),
                  pl.BlockSpec(perm.shape, lambda bi, i: (0, 0, 0), pipeline_mode=pl.Buffered(1))],
        out_specs=pl.BlockSpec(blk, lambda bi, i: (bi, i, 0)),
        out_shape=jax.ShapeDtypeStruct((b, s, ATT_WIDTH), BF16),
        scratch_shapes=(
            [pltpu.VMEM((rows, ATT_SLAB), BF16)
             for d in DILATIONS
             for rows in (ATT_TILE, d * _att_geometry(d)[2], d * _att_geometry(d)[2])]
            + [pltpu.VMEM((2, ATT_TRIP_QUERIES // ATT_GQ, 2 * ATT_GQ, ATT_GQ + 2 * HALF_WINDOW), F32)]
            + [pltpu.VMEM((ATT_TILE, 128), F32) for _ in range(3 * len(DILATIONS))]),
        compiler_params=_cparams(("arbitrary", "arbitrary")),
        name="attn",
    )(z, z, z, z, z, z, z, bias, perm)


def _silu(t):
    return t * (0.5 + 0.5 * jnp.tanh(0.5 * t))


def _out_kernel(x_ref, of_ref, ob_ref, g_ref, att_ref, mod_ref, gnw_ref, anw_ref, n2w_ref, fnw_ref,
                wo_ref, wg_ref, wu_ref, wd_ref, out_ref, h2_ref, acc_ref, act_ref):
    x = x_ref[0]
    o = of_ref[0].astype(F32) + ob_ref[0].astype(F32)
    gate = _silu(g_ref[0].astype(F32))
    parts = []
    for h in range(HG_HEADS):
        cols = slice(h * HG_KEY, (h + 1) * HG_KEY)
        oh = o[:, cols]
        ms = jnp.mean(oh * oh, axis=-1, keepdims=True)
        parts.append(oh * lax.rsqrt(ms + EPS) * gnw_ref[:, cols] * gate[:, cols])
    att = att_ref[0].astype(F32)
    ms = jnp.mean(att * att, axis=-1, keepdims=True)
    parts.append(att * lax.rsqrt(ms + EPS) * anw_ref[...])
    y = jnp.concatenate(parts, axis=-1).astype(BF16)
    mix = jnp.dot(y, wo_ref[...], preferred_element_type=F32)
    x1 = x + mod_ref[0, 2:3, :] * mix

    ms = jnp.mean(x1 * x1, axis=-1, keepdims=True)
    h2 = x1 * lax.rsqrt(ms + EPS) * n2w_ref[...]
    h2_ref[...] = (h2 * (1.0 + mod_ref[0, 4:5, :]) + mod_ref[0, 3:4, :]).astype(BF16)

    nchunks = D_FF // FF_CHUNK

    def act_chunk(j):
        c0 = pl.multiple_of(j * FF_CHUNK, FF_CHUNK)
        h2b = h2_ref[...]
        gt = jnp.dot(h2b, wg_ref[:, pl.ds(c0, FF_CHUNK)], preferred_element_type=F32)
        up = jnp.dot(h2b, wu_ref[:, pl.ds(c0, FF_CHUNK)], preferred_element_type=F32)
        return gt, up

    def down(j):
        c0 = pl.multiple_of(j * FF_CHUNK, FF_CHUNK)
        return jnp.dot(act_ref[j % 2], wd_ref[pl.ds(c0, FF_CHUNK), :], preferred_element_type=F32)

    gt, up = act_chunk(0)
    act_ref[0] = (_silu(gt) * up).astype(BF16)
    acc_ref[...] = jnp.zeros_like(acc_ref)

    def ff_body(j, carry):
        gt, up = act_chunk(j + 1)
        acc_ref[...] += down(j)
        act_ref[(j + 1) % 2] = (_silu(gt) * up).astype(BF16)
        return carry

    lax.fori_loop(0, nchunks - 1, ff_body, 0, unroll=True)
    x2 = x1 + mod_ref[0, 5:6, :] * (acc_ref[...] + down(nchunks - 1))
    ms = jnp.mean(x2 * x2, axis=-1, keepdims=True)
    out_ref[0] = x2 * lax.rsqrt(ms + EPS) * fnw_ref[...]


def _out(x, o_f, o_b, z, att, mod, gn_w, an_w, n2_w, fn_w, wo, wg, wu, wd):
    b, s, d = x.shape

    def tok(width, col=0):
        return pl.BlockSpec((1, TM_OUT, width), lambda bi, i: (bi, i, col))

    def const(shape):
        return pl.BlockSpec(shape, lambda bi, i: (0,) * len(shape), pipeline_mode=pl.Buffered(1))

    return pl.pallas_call(
        _out_kernel,
        grid=(b, s // TM_OUT),
        in_specs=[tok(d), tok(HG_WIDTH), tok(HG_WIDTH), tok(HG_WIDTH, COL_HG), tok(ATT_WIDTH),
                  pl.BlockSpec((1, 6, d), lambda bi, i: (bi, 0, 0)),
                  const((1, HG_WIDTH)), const((1, ATT_WIDTH)), const((1, d)), const((1, d)),
                  const(wo.shape), const(wg.shape), const(wu.shape), const(wd.shape)],
        out_specs=tok(d),
        out_shape=jax.ShapeDtypeStruct((b, s, d), F32),
        scratch_shapes=[pltpu.VMEM((TM_OUT, d), BF16), pltpu.VMEM((TM_OUT, d), F32),
                        pltpu.VMEM((2, TM_OUT, FF_CHUNK), BF16)],
        compiler_params=_cparams(("arbitrary", "arbitrary")),
        name="out_ffn",
    )(x, o_f, o_b, z, att, mod, gn_w, an_w, n2_w, fn_w, wo, wg, wu, wd)


def _cumsum_matrices():
    t = np.arange(HG_CHUNK)
    fwd = (t[None, :] <= t[:, None]).astype(np.float32)
    bwd = (t[None, :] >= t[:, None]).astype(np.float32)
    return jnp.asarray(np.stack([fwd, bwd]), dtype=BF16)


def kernel(x, c, rel_bias, w_ada, b_ada, norm1_w, w_in, hg_lower_bound, hg_norm_w, attn_norm_w,
           w_out, norm2_w, w_gate, w_up, w_down, final_norm_w):
    b, s, d = x.shape
    assert d == D_MODEL and s % ATT_TILE == 0 and s % HG_BLOCK == 0 and w_ada.shape[0] == 1
    c_pad = jnp.zeros((8, d), F32).at[:b].set(c)
    mod = _ada(c_pad, w_ada[0], b_ada[0][None, :])[:b].reshape(b, 6, d)

    bias = _bias_tiles(rel_bias.astype(F32), jnp.asarray(_bucket_tiles(ATT_GQ)))
    bias = bias.reshape(len(DILATIONS), ATT_PAIRS, 2 * ATT_GQ, ATT_GQ + 2 * HALF_WINDOW)
    z = _inproj(x, mod, norm1_w[0][None, :], w_in[0].astype(BF16))
    o_f, o_b = _hgrn(z, hg_lower_bound.astype(F32), _cumsum_matrices())
    att = _attn(z, bias)
    return _out(x, o_f, o_b, z, att, mod,
                hg_norm_w[0][None, :], attn_norm_w[0][None, :], norm2_w[0][None, :],
                final_norm_w[None, :],
                w_out[0].astype(BF16), w_gate[0].astype(BF16), w_up[0].astype(BF16),
                w_down[0].astype(BF16))
```

```python
import functools
import math

import numpy as np
import jax
import jax.numpy as jnp
from jax import lax
from jax.experimental import pallas as pl
from jax.experimental.pallas import tpu as pltpu

F32 = jnp.float32
BF16 = jnp.bfloat16

D_MODEL = 1024
HG_HEADS = 4
HG_KEY = 128
HG_WIDTH = HG_HEADS * HG_KEY
ATT_HEADS = 8
ATT_HEAD_DIM = 64
ATT_WIDTH = ATT_HEADS * ATT_HEAD_DIM
DILATIONS = (1, 4, 16)
HALF_WINDOW = 64
NUM_BUCKETS = 32
MAX_DISTANCE = 1024
D_FF = 2816
EPS = 1e-6
IN_WIDTH = 5 * HG_WIDTH + 3 * ATT_WIDTH
COL_HQ, COL_HFF, COL_HFB, COL_HI, COL_HG, COL_AQ, COL_AK, COL_AV = range(8)

VMEM_LIMIT_BYTES = 56 * 1024 * 1024

TM_IN = 1024
TM_OUT = 512
FF_CHUNK = 256
HG_CHUNK = 128
HG_SUB = 64
HG_BLOCK = 1024
ATT_TILE = 2048
ATT_HALO = HALF_WINDOW * max(DILATIONS)
ATT_GQ = 128
ATT_TRIP_QUERIES = 1024
NEG = -1e30
LOG2E = 1.4426950408889634


def _cparams(sem):
    return pltpu.CompilerParams(dimension_semantics=sem, vmem_limit_bytes=VMEM_LIMIT_BYTES)


def _ada_kernel(c_ref, w_ref, b_ref, o_ref):
    c = c_ref[...]
    sc = c * (1.0 / (1.0 + jnp.exp(-c)))
    rows = sc.shape[0]
    s_hi = sc.astype(BF16)
    s_lo = (sc - s_hi.astype(F32)).astype(BF16)
    w = w_ref[...]
    w_hi = w.astype(BF16)
    w_lo = (w - w_hi.astype(F32)).astype(BF16)
    p = jnp.dot(jnp.concatenate([s_hi, s_lo], axis=0), w_hi, preferred_element_type=F32)
    q = jnp.dot(s_hi, w_lo, preferred_element_type=F32)
    o_ref[...] = p[:rows] + p[rows:] + q + b_ref[...]


def _ada(c_pad, w_ada, b_ada):
    rows, d = c_pad.shape
    n = w_ada.shape[1]
    bn = 1536
    return pl.pallas_call(
        _ada_kernel,
        grid=(n // bn,),
        in_specs=[pl.BlockSpec((rows, d), lambda j: (0, 0)),
                  pl.BlockSpec((d, bn), lambda j: (0, j)),
                  pl.BlockSpec((1, bn), lambda j: (0, j))],
        out_specs=pl.BlockSpec((rows, bn), lambda j: (0, j)),
        out_shape=jax.ShapeDtypeStruct((rows, n), F32),
        compiler_params=_cparams(("arbitrary",)),
        name="ada",
    )(c_pad, w_ada, b_ada)


def _t5_bucket_np(rel):
    half = NUM_BUCKETS // 2
    max_exact = half // 2
    ret = np.where(rel > 0, half, 0)
    n = np.abs(rel)
    nf = np.maximum(n, 1).astype(np.float64)
    large = max_exact + (np.log(nf / max_exact) / math.log(MAX_DISTANCE / max_exact)
                         * (half - max_exact)).astype(np.int64)
    large = np.minimum(large, half - 1)
    return (ret + np.where(n < max_exact, n, large)).astype(np.int32)


def _bucket_tiles(gq):
    gk = gq + 2 * HALF_WINDOW
    off = (np.arange(gk)[None, :] - HALF_WINDOW) - np.arange(gq)[:, None]
    tiles = []
    for d in DILATIONS:
        tiles.append(np.where(np.abs(off) <= HALF_WINDOW, _t5_bucket_np(off * d), -1))
    return np.stack(tiles).astype(np.int32)


def _bias_kernel(rb_ref, bk_ref, o_ref):
    bk = bk_ref[0]
    for h in range(ATT_HEADS):
        acc = jnp.full(bk.shape, NEG, F32)
        for b in range(NUM_BUCKETS):
            acc = jnp.where(bk == b, rb_ref[b, h] * LOG2E, acc)
        o_ref[0, h] = acc


def _bias_tiles(rel_bias, buckets):
    npat, gq, gk = buckets.shape
    return pl.pallas_call(
        _bias_kernel,
        grid=(npat,),
        in_specs=[pl.BlockSpec(memory_space=pltpu.SMEM),
                  pl.BlockSpec((1, gq, gk), lambda p: (p, 0, 0))],
        out_specs=pl.BlockSpec((1, ATT_HEADS, gq, gk), lambda p: (p, 0, 0, 0)),
        out_shape=jax.ShapeDtypeStruct((npat, ATT_HEADS, gq, gk), F32),
        compiler_params=_cparams(("arbitrary",)),
        name="bias_tiles",
    )(rel_bias, buckets)


def _inproj_kernel(x_ref, mod_ref, nw_ref, w_ref, z_ref):
    x = x_ref[0]
    ms = jnp.mean(x * x, axis=-1, keepdims=True)
    y = x * lax.rsqrt(ms + EPS) * nw_ref[...]
    shift = mod_ref[0, 0:1, :]
    scale = mod_ref[0, 1:2, :]
    h = (y * (1.0 + scale) + shift).astype(BF16)
    for j in range(IN_WIDTH // 512):
        cols = slice(j * 512, (j + 1) * 512)
        z_ref[0, :, cols] = jnp.dot(h, w_ref[:, cols], preferred_element_type=F32).astype(BF16)


def _inproj(x, mod, norm_w, w_in_bf16):
    b, s, d = x.shape
    return pl.pallas_call(
        _inproj_kernel,
        grid=(b, s // TM_IN),
        in_specs=[pl.BlockSpec((1, TM_IN, d), lambda bi, i: (bi, i, 0)),
                  pl.BlockSpec((1, 6, d), lambda bi, i: (bi, 0, 0)),
                  pl.BlockSpec((1, d), lambda bi, i: (0, 0)),
                  pl.BlockSpec((d, IN_WIDTH), lambda bi, i: (0, 0), pipeline_mode=pl.Buffered(1))],
        out_specs=pl.BlockSpec((1, TM_IN, IN_WIDTH), lambda bi, i: (bi, i, 0)),
        out_shape=jax.ShapeDtypeStruct((b, s, IN_WIDTH), BF16),
        compiler_params=_cparams(("arbitrary", "arbitrary")),
        name="inproj",
    )(x, mod, norm_w, w_in_bf16)


_NT =(((1,), (1,)), ((), ()))
_TN = (((0,), (0,)), ((), ()))


def _hgrn_gates(z, lb, log2_one_m_lb):
    e = jnp.exp(-jnp.abs(z))
    ope = 1.0 + e
    l1 = jnp.log2(ope)
    w = jnp.where(z >= 0, 1.0, e)
    logf = jnp.log2(lb * ope + (1.0 - lb) * w) - l1
    logk = log2_one_m_lb - jnp.maximum(z, 0.0) * LOG2E - l1
    return logf, logk


def _hgrn_decays(q, logf, logk, tri, reverse):
    L = HG_CHUNK
    hi = logf.astype(BF16)
    lo = (logf - hi.astype(F32)).astype(BF16)
    res = jnp.dot(tri, jnp.concatenate([hi, lo], axis=1), preferred_element_type=F32)
    g = res[:, :HG_WIDTH] + res[:, HG_WIDTH:]

    nsub = L // HG_SUB
    ref_off = HG_SUB // 2 if reverse else HG_SUB // 2 - 1
    grefs = [g[i * HG_SUB + ref_off:i * HG_SUB + ref_off + 1, :] for i in range(nsub)]
    gref_full = jnp.concatenate([jnp.broadcast_to(r, (HG_SUB, HG_WIDTH)) for r in grefs], axis=0)
    egref_full = jnp.concatenate([jnp.broadcast_to(jnp.exp2(r), (HG_SUB, HG_WIDTH)) for r in grefs],
                                 axis=0)
    qq32 = q * jnp.exp2(g - gref_full)
    qq = qq32.astype(BF16)
    qg = (qq32 * egref_full).astype(BF16)

    lkg = logk - g
    g_end = g[0:1, :] if reverse else g[L - 1:L, :]
    kk = []
    for i in range(nsub):
        lo_r, hi_r = (i * HG_SUB, L) if reverse else (0, (i + 1) * HG_SUB)
        part = jnp.exp2(lkg[lo_r:hi_r] + grefs[i])
        if hi_r - lo_r == L:
            kd = (part * jnp.exp2(g_end - grefs[i])).astype(BF16)
        pad = jnp.zeros((L - (hi_r - lo_r), HG_WIDTH), BF16)
        pieces = [pad, part.astype(BF16)] if reverse else [part.astype(BF16), pad]
        kk.append(jnp.concatenate(pieces, axis=0) if pad.shape[0] else part.astype(BF16))
    return qq, kk, qg, kd, jnp.exp2(g_end)


def _hgrn_kernel(qf_ref, ff_ref, vf_ref, qb_ref, fb_ref, vb_ref, lbp_ref, tri_ref,
                 of_ref, ob_ref, stf_ref, stb_ref):
    @pl.when(pl.program_id(1) == 0)
    def _():
        stf_ref[...] = jnp.zeros_like(stf_ref)
        stb_ref[...] = jnp.zeros_like(stb_ref)

    lbp = lbp_ref[...]
    lbe = jnp.exp(lbp - jnp.max(lbp, axis=0, keepdims=True))
    lb_all = lbe[0] / jnp.sum(lbe, axis=0)
    log2_one_m_lb = jnp.log2(1.0 - lb_all)

    L = HG_CHUNK
    nchunk = HG_BLOCK // L
    nsub = L // HG_SUB
    t_idx = lax.broadcasted_iota(jnp.int32, (L, L), 0)
    s_idx = lax.broadcasted_iota(jnp.int32, (L, L), 1)
    dirs = ((qf_ref, ff_ref, vf_ref, of_ref, stf_ref, False),
            (qb_ref, fb_ref, vb_ref, ob_ref, stb_ref, True))

    units = [(di, h) for h in range(HG_HEADS) for di in range(2)]

    def chunk_rows(ci):
        return (pl.ds(pl.multiple_of(ci * L, L), L),
                pl.ds(pl.multiple_of((nchunk - 1 - ci) * L, L), L))

    def body(ci, carry):
        rows = chunk_rows(ci)
        gates = [_hgrn_gates(f_ref[0, rows[di], :].astype(F32), lb_all[di:di + 1],
                             log2_one_m_lb[di:di + 1])
                 for di, (_, f_ref, _, _, _, _) in enumerate(dirs)]
        dec = [_hgrn_decays(q_ref[0, rows[di], :].astype(F32), gates[di][0], gates[di][1],
                            tri_ref[di], rev)
               for di, (q_ref, _, _, _, _, rev) in enumerate(dirs)]
        scores = {}
        for di, h in units:
            cols = slice(h * HG_KEY, (h + 1) * HG_KEY)
            qq, kk = dec[di][0], dec[di][1]
            scores[di, h] = jnp.concatenate(
                [lax.dot_general(qq[i * HG_SUB:(i + 1) * HG_SUB, cols], kk[i][:, cols], _NT,
                                 preferred_element_type=F32) for i in range(nsub)], axis=0)
        for di, h in units:
            cols = slice(h * HG_KEY, (h + 1) * HG_KEY)
            _, _, v_ref, o_ref, st_ref, rev = dirs[di]
            _, _, qg, kd, e_end = dec[di]
            causal = (s_idx >= t_idx) if rev else (s_idx <= t_idx)
            a = jnp.where(causal, scores[di, h], 0.0).astype(BF16)
            vh = v_ref[0, rows[di], cols]
            st = st_ref[h]
            o = (jnp.dot(a, vh, preferred_element_type=F32)
                 + lax.dot_general(qg[:, cols], st.astype(BF16), _NT, preferred_element_type=F32))
            o_ref[0, rows[di], cols] = o.astype(BF16)
            st_ref[h] = (st * e_end[:, cols]
                         + lax.dot_general(vh, kd[:, cols], _TN, preferred_element_type=F32))
        return carry

    lax.fori_loop(0, nchunk, body, 0)


def _hgrn(z, lb_params, tri):
    b, s, _ = z.shape
    nb = s // HG_BLOCK

    def fwd(col):
        return pl.BlockSpec((1, HG_BLOCK, HG_WIDTH), lambda bi, j: (bi, j, col))

    def bwd(col):
        return pl.BlockSpec((1, HG_BLOCK, HG_WIDTH), lambda bi, j: (bi, nb - 1 - j, col))

    out = jax.ShapeDtypeStruct((b, s, HG_WIDTH), BF16)
    return pl.pallas_call(
        _hgrn_kernel,
        grid=(b, nb),
        in_specs=[fwd(COL_HQ), fwd(COL_HFF), fwd(COL_HI),
                  bwd(COL_HQ), bwd(COL_HFB), bwd(COL_HI),
                  pl.BlockSpec(lb_params.shape, lambda bi, j: (0, 0, 0)),
                  pl.BlockSpec((2, HG_CHUNK, HG_CHUNK), lambda bi, j: (0, 0, 0))],
        out_specs=[pl.BlockSpec((1, HG_BLOCK, HG_WIDTH), lambda bi, j: (bi, j, 0)),
                   pl.BlockSpec((1, HG_BLOCK, HG_WIDTH), lambda bi, j: (bi, nb - 1 - j, 0))],
        out_shape=[out, out],
        scratch_shapes=[pltpu.VMEM((HG_HEADS, HG_KEY, HG_KEY), F32),
                        pltpu.VMEM((HG_HEADS, HG_KEY, HG_KEY), F32)],
        compiler_params=_cparams(("arbitrary", "arbitrary")),
        name="hgrn",
    )(z, z, z, z, z, z, lb_params, tri)


ATT_PAIRS = ATT_HEADS // 2


def _att_geometry(d):
    nq = ATT_TILE // d
    assert nq % ATT_GQ == 0
    return nq, nq // ATT_GQ, nq + 2 * HALF_WINDOW


ATT_SLAB = 256
ATT_PERM_ROWS = 256


def _perm_matrices():
    n = ATT_PERM_ROWS
    mats = []
    for d in DILATIONS[1:]:
        p = np.zeros((n, n), np.float32)
        i = np.arange(n)
        p[i, d * (i % (n // d)) + i // (n // d)] = 1.0
        mats.append(p)
    return jnp.asarray(np.stack(mats), dtype=BF16)


ATT_REGROUP_BATCH = 8


def _attn_regroup(jobs, perm_ref, lanes):
    for i in range(0, len(jobs), ATT_REGROUP_BATCH):
        batch = jobs[i:i + ATT_REGROUP_BATCH]
        ys = [jnp.dot(perm_ref[p - 1], src_ref[0, row:row + ATT_PERM_ROWS, lanes],
                      preferred_element_type=F32)
              for p, d, src_ref, row, _, _, _, _ in batch]
        for (p, d, _, _, dst_ref, m0, rows_per_res, scale), y in zip(batch, ys):
            bm = ATT_PERM_ROWS // d
            if scale is not None:
                y = y * scale
            y = y.astype(BF16)
            for r in range(d):
                dst_ref[r * rows_per_res + m0:r * rows_per_res + m0 + bm, :] = y[r * bm:(r + 1) * bm]


def _attn_kernel(q_ref, kp_ref, kc_ref, kn_ref, vp_ref, vc_ref, vn_ref, bias_ref, perm_ref, o_ref,
                 q1_ref, k1_ref, v1_ref, q4_ref, k4_ref, v4_ref, q16_ref, k16_ref, v16_ref,
                 sc_ref, *part_refs, seq_len):
    T = ATT_TILE
    gq, gk = ATT_GQ, ATT_GQ + 2 * HALF_WINDOW
    t0 = pl.program_id(1) * T
    q_res = (q1_ref, q4_ref, q16_ref)
    k_res = (k1_ref, k4_ref, k16_ref)
    v_res = (v1_ref, v4_ref, v16_ref)
    parts = [part_refs[3 * p:3 * p + 3] for p in range(len(DILATIONS))]
    lane = lax.broadcasted_iota(jnp.int32, (1, 128), 1)
    first = lane < ATT_HEAD_DIM
    jk = lax.broadcasted_iota(jnp.int32, (1, gk), 1)
    q_scale = ATT_HEAD_DIM ** -0.5 * LOG2E
    kv_srcs = ((k_res, (kp_ref, kc_ref, kn_ref)), (v_res, (vp_ref, vc_ref, vn_ref)))

    def slab_body(slab, carry):
        sl = pl.ds(pl.multiple_of(slab * ATT_SLAB, ATT_SLAB), ATT_SLAB)
        q1_ref[...] = (q_ref[0, :, sl].astype(F32) * q_scale).astype(BF16)
        for res, (prv, cur, nxt) in kv_srcs:
            res[0][0:HALF_WINDOW, :] = prv[0, ATT_HALO - HALF_WINDOW:ATT_HALO, sl]
            res[0][HALF_WINDOW:HALF_WINDOW + T, :] = cur[0, :, sl]
            res[0][HALF_WINDOW + T:, :] = nxt[0, 0:HALF_WINDOW, sl]
        jobs = []
        for p, d in enumerate(DILATIONS):
            if p == 0:
                continue
            nq, _, nk = _att_geometry(d)
            halo = d * HALF_WINDOW
            bm = ATT_PERM_ROWS // d

            def blocks(src, row0, ntok, dst, m0, rows_per_res, scale, p=p, d=d, bm=bm):
                return [(p, d, src, row0 + b * ATT_PERM_ROWS, dst, m0 + b * bm, rows_per_res, scale)
                        for b in range(ntok // ATT_PERM_ROWS)]

            jobs += blocks(q_ref, 0, T, q_res[p], 0, nq, q_scale)
            for res, (prv, cur, nxt) in kv_srcs:
                jobs += blocks(prv, ATT_HALO - halo, halo, res[p], 0, nk, None)
                jobs += blocks(cur, 0, T, res[p], HALF_WINDOW, nk, None)
                jobs += blocks(nxt, 0, halo, res[p], HALF_WINDOW + nq, nk, None)
        _attn_regroup(jobs, perm_ref, sl)

        def pair_body(j, carry):
            pair = slab * (ATT_SLAB // 128) + j
            lanes = pl.ds(pl.multiple_of(j * 128, 128), 128)
            trips = [_AttnTrip(p, d, ti, pair, lanes, t0, seq_len, first, jk, q_res[p], k_res[p],
                               v_res[p], bias_ref, parts[p])
                     for p, d in enumerate(DILATIONS) for ti in range(T // ATT_TRIP_QUERIES)]
            for sub in range(trips[0].n):
                sc_ref[0, sub] = trips[0].scores(sub)
            for k, trip in enumerate(trips):
                slot = k % 2
                soft = []
                for sub in range(trip.n):
                    if k + 1 < len(trips):
                        sc_ref[1 - slot, sub] = trips[k + 1].scores(sub)
                    soft.append(trip.softmax(sub, sc_ref[slot, sub]))
                for sub in range(trip.n):
                    trip.output(sub, *soft[sub])
            for p in range(len(DILATIONS) - 1, 0, -1):
                _attn_merge(parts[p - 1], parts[p], DILATIONS[p - 1], DILATIONS[p])
            o_ref[0, :, pl.ds(pl.multiple_of(pair * 128, 128), 128)] = \
                (parts[0][0][...] / parts[0][2][...]).astype(BF16)
            return carry

        lax.fori_loop(0, ATT_SLAB // 128, pair_body, 0)
        return carry

    lax.fori_loop(0, ATT_WIDTH // ATT_SLAB, slab_body, 0)


def _attn_merge(run, pat, d_run, d_pat):
    accn_ref, mn_ref, ln_ref = run
    accp_ref, mp_ref, lp_ref = pat
    nq = ATT_TILE // d_pat
    for r in range(d_pat):
        nat = (pl.ds((r % d_run) * (ATT_TILE // d_run) + r // d_run, nq, stride=d_pat // d_run),
               slice(None))
        res = (slice(r * nq, (r + 1) * nq), slice(None))
        m_old = mn_ref[nat]
        m_pat = mp_ref[res]
        m_new = jnp.maximum(m_old, m_pat)
        a_old = jnp.exp2(m_old - m_new)
        a_pat = jnp.exp2(m_pat - m_new)
        accn_ref[nat] = accn_ref[nat] * a_old + accp_ref[res] * a_pat
        ln_ref[nat] = ln_ref[nat] * a_old + lp_ref[res] * a_pat
        mn_ref[nat] = m_new


class _AttnTrip:
    def __init__(self, p, d, ti, pair, lanes, t0, seq_len, first, jk, q_ref, k_ref, v_ref, bias_ref,
                 dst):
        nq, gpr, nk = _att_geometry(d)
        self.gq, self.gk = ATT_GQ, ATT_GQ + 2 * HALF_WINDOW
        self.n = ATT_TRIP_QUERIES // ATT_GQ
        self.p, self.pair, self.lanes, self.first = p, pair, lanes, first
        self.q_ref, self.k_ref, self.v_ref, self.bias_ref, self.dst = q_ref, k_ref, v_ref, bias_ref, dst
        self.q0, self.k0, self.kvalid = [], [], []
        for sub in range(self.n):
            gi = ti * self.n + sub
            r, g = gi // gpr, gi % gpr
            self.q0.append(gi * ATT_GQ)
            self.k0.append(r * nk + g * ATT_GQ)
            ktok = t0 + r + d * (g * ATT_GQ - HALF_WINDOW + jk)
            self.kvalid.append((ktok >= 0) & (ktok < seq_len))

    def scores(self, sub):
        qg = self.q_ref[self.q0[sub]:self.q0[sub] + self.gq, self.lanes]
        kg = self.k_ref[self.k0[sub]:self.k0[sub] + self.gk, self.lanes]
        zero = jnp.zeros_like(qg)
        q2 = jnp.concatenate([jnp.where(self.first, qg, zero), jnp.where(self.first, zero, qg)],
                             axis=0)
        return lax.dot_general(q2, kg, _NT, preferred_element_type=F32)

    def softmax(self, sub, sc):
        b2 = self.bias_ref[self.p, self.pair]
        sc = jnp.where(self.kvalid[sub], sc + b2, NEG)
        mx = jnp.max(sc, axis=-1, keepdims=True)
        return jnp.exp2(sc - mx), mx

    def output(self, sub, pe, mx):
        gq = self.gq
        ls = jnp.sum(pe, axis=-1, keepdims=True)
        vg = self.v_ref[self.k0[sub]:self.k0[sub] + self.gk, self.lanes]
        pv = jnp.dot(pe.astype(BF16), vg, preferred_element_type=F32)
        rows = (slice(self.q0[sub], self.q0[sub] + gq), slice(None))
        self.dst[0][rows] = jnp.where(self.first, pv[0:gq], pv[gq:2 * gq])
        self.dst[1][rows] = jnp.where(self.first, mx[0:gq], mx[gq:2 * gq])
        self.dst[2][rows] = jnp.where(self.first, ls[0:gq], ls[gq:2 * gq])


def _attn(z, bias):
    b, s, _ = z.shape
    perm = _perm_matrices()
    nt = s // ATT_TILE
    blk = (1, ATT_TILE, ATT_WIDTH)
    halo_blk = (1, ATT_HALO, ATT_WIDTH)
    hpt = ATT_TILE // ATT_HALO
    nh = s // ATT_HALO

    def cur(col):
        return pl.BlockSpec(blk, lambda bi, i: (bi, i, col))

    def prev(col):
        return pl.BlockSpec(halo_blk, lambda bi, i: (bi, jnp.maximum(i * hpt - 1, 0), col))

    def nxt(col):
        return pl.BlockSpec(halo_blk, lambda bi, i: (bi, jnp.minimum((i + 1) * hpt, nh - 1), col))

    return pl.pallas_call(
        functools.partial(_attn_kernel, seq_len=s),
        grid=(b, nt),
        in_specs=[cur(COL_AQ), prev(COL_AK), cur(COL_AK), nxt(COL_AK),
                  prev(COL_AV), cur(COL_AV), nxt(COL_AV),
                  pl.BlockSpec(bias.shape, lambda bi, i: (0, 0, 0, 0), pipeline_mode=pl.Buffered(1)),
                  pl.BlockSpec(perm.shape, lambda bi, i: (0, 0, 0), pipeline_mode=pl.Buffered(1))],
        out_specs=pl.BlockSpec(blk, lambda bi, i: (bi, i, 0)),
        out_shape=jax.ShapeDtypeStruct((b, s, ATT_WIDTH), BF16),
        scratch_shapes=(
            [pltpu.VMEM((rows, ATT_SLAB), BF16)
             for d in DILATIONS
             for rows in (ATT_TILE, d * _att_geometry(d)[2], d * _att_geometry(d)[2])]
            + [pltpu.VMEM((2, ATT_TRIP_QUERIES // ATT_GQ, 2 * ATT_GQ, ATT_GQ + 2 * HALF_WINDOW), F32)]
            + [pltpu.VMEM((ATT_TILE, 128), F32) for _ in range(3 * len(DILATIONS))]),
        compiler_params=_cparams(("arbitrary", "arbitrary")),
        name="attn",
    )(z, z, z, z, z, z, z, bias, perm)


def _silu(t):
    return t * (0.5 + 0.5 * jnp.tanh(0.5 * t))


def _out_kernel(x_ref, of_ref, ob_ref, g_ref, att_ref, mod_ref, gnw_ref, anw_ref, n2w_ref, fnw_ref,
                wo_ref, wg_ref, wu_ref, wd_ref, out_ref, h2_ref, acc_ref, act_ref):
    x = x_ref[0]
    o = of_ref[0].astype(F32) + ob_ref[0].astype(F32)
    gate = _silu(g_ref[0].astype(F32))
    parts = []
    for h in range(HG_HEADS):
        cols = slice(h * HG_KEY, (h + 1) * HG_KEY)
        oh = o[:, cols]
        ms = jnp.mean(oh * oh, axis=-1, keepdims=True)
        parts.append(oh * lax.rsqrt(ms + EPS) * gnw_ref[:, cols] * gate[:, cols])
    att = att_ref[0].astype(F32)
    ms = jnp.mean(att * att, axis=-1, keepdims=True)
    parts.append(att * lax.rsqrt(ms + EPS) * anw_ref[...])
    y = jnp.concatenate(parts, axis=-1).astype(BF16)
    mix = jnp.dot(y, wo_ref[...], preferred_element_type=F32)
    x1 = x + mod_ref[0, 2:3, :] * mix

    ms = jnp.mean(x1 * x1, axis=-1, keepdims=True)
    h2 = x1 * lax.rsqrt(ms + EPS) * n2w_ref[...]
    h2_ref[...] = (h2 * (1.0 + mod_ref[0, 4:5, :]) + mod_ref[0, 3:4, :]).astype(BF16)

    nchunks = D_FF // FF_CHUNK

    def act_chunk(j):
        c0 = pl.multiple_of(j * FF_CHUNK, FF_CHUNK)
        h2b = h2_ref[...]
        gt = jnp.dot(h2b, wg_ref[:, pl.ds(c0, FF_CHUNK)], preferred_element_type=F32)
        up = jnp.dot(h2b, wu_ref[:, pl.ds(c0, FF_CHUNK)], preferred_element_type=F32)
        return gt, up

    def down(j):
        c0 = pl.multiple_of(j * FF_CHUNK, FF_CHUNK)
        return jnp.dot(act_ref[j % 2], wd_ref[pl.ds(c0, FF_CHUNK), :], preferred_element_type=F32)

    gt, up = act_chunk(0)
    act_ref[0] = (_silu(gt) * up).astype(BF16)
    acc_ref[...] = jnp.zeros_like(acc_ref)

    def ff_body(j, carry):
        gt, up = act_chunk(j + 1)
        acc_ref[...] += down(j)
        act_ref[(j + 1) % 2] = (_silu(gt) * up).astype(BF16)
        return carry

    lax.fori_loop(0, nchunks - 1, ff_body, 0, unroll=True)
    x2 = x1 + mod_ref[0, 5:6, :] * (acc_ref[...] + down(nchunks - 1))
    ms = jnp.mean(x2 * x2, axis=-1, keepdims=True)
    out_ref[0] = x2 * lax.rsqrt(ms + EPS) * fnw_ref[...]


def _out(x, o_f, o_b, z, att, mod, gn_w, an_w, n2_w, fn_w, wo, wg, wu, wd):
    b, s, d = x.shape

    def tok(width, col=0):
        return pl.BlockSpec((1, TM_OUT, width), lambda bi, i: (bi, i, col))

    def const(shape):
        return pl.BlockSpec(shape, lambda bi, i: (0,) * len(shape), pipeline_mode=pl.Buffered(1))

    return pl.pallas_call(
        _out_kernel,
        grid=(b, s // TM_OUT),
        in_specs=[tok(d), tok(HG_WIDTH), tok(HG_WIDTH), tok(HG_WIDTH, COL_HG), tok(ATT_WIDTH),
                  pl.BlockSpec((1, 6, d), lambda bi, i: (bi, 0, 0)),
                  const((1, HG_WIDTH)), const((1, ATT_WIDTH)), const((1, d)), const((1, d)),
                  const(wo.shape), const(wg.shape), const(wu.shape), const(wd.shape)],
        out_specs=tok(d),
        out_shape=jax.ShapeDtypeStruct((b, s, d), F32),
        scratch_shapes=[pltpu.VMEM((TM_OUT, d), BF16), pltpu.VMEM((TM_OUT, d), F32),
                        pltpu.VMEM((2, TM_OUT, FF_CHUNK), BF16)],
        compiler_params=_cparams(("arbitrary", "arbitrary")),
        name="out_ffn",
    )(x, o_f, o_b, z, att, mod, gn_w, an_w, n2_w, fn_w, wo, wg, wu, wd)


def _cumsum_matrices():
    t = np.arange(HG_CHUNK)
    fwd = (t[None, :] <= t[:, None]).astype(np.float32)
    bwd = (t[None, :] >= t[:, None]).astype(np.float32)
    return jnp.asarray(np.stack([fwd, bwd]), dtype=BF16)


def kernel(x, c, rel_bias, w_ada, b_ada, norm1_w, w_in, hg_lower_bound, hg_norm_w, attn_norm_w,
           w_out, norm2_w, w_gate, w_up, w_down, final_norm_w):
    b, s, d = x.shape
    assert d == D_MODEL and s % ATT_TILE == 0 and s % HG_BLOCK == 0 and w_ada.shape[0] == 1
    c_pad = jnp.zeros((16, d), F32).at[:b].set(c)
    mod = _ada(c_pad, w_ada[0], b_ada[0][None, :])[:b].reshape(b, 6, d)

    bias = _bias_tiles(rel_bias.astype(F32), jnp.asarray(_bucket_tiles(ATT_GQ)))
    bias = bias.reshape(len(DILATIONS), ATT_PAIRS, 2 * ATT_GQ, ATT_GQ + 2 * HALF_WINDOW)
    z = _inproj(x, mod, norm1_w[0][None, :], w_in[0].astype(BF16))
    o_f, o_b = _hgrn(z, hg_lower_bound.astype(F32), _cumsum_matrices())
    att = _attn(z, bias)
    return _out(x, o_f, o_b, z, att, mod,
                hg_norm_w[0][None, :], attn_norm_w[0][None, :], norm2_w[0][None, :],
                final_norm_w[None, :],
                w_out[0].astype(BF16), w_gate[0].astype(BF16), w_up[0].astype(BF16),
                w_down[0].astype(BF16))
```

```python
import functools
import math

import numpy as np
import jax
import jax.numpy as jnp
from jax import lax
from jax.experimental import pallas as pl
from jax.experimental.pallas import tpu as pltpu

F32 = jnp.float32
BF16 = jnp.bfloat16

D_MODEL = 1024
HG_HEADS = 4
HG_KEY = 128
HG_WIDTH = HG_HEADS * HG_KEY
ATT_HEADS = 8
ATT_HEAD_DIM = 64
ATT_WIDTH = ATT_HEADS * ATT_HEAD_DIM
DILATIONS = (1, 4, 16)
HALF_WINDOW = 64
NUM_BUCKETS = 32
MAX_DISTANCE = 1024
D_FF = 2816
EPS = 1e-6
IN_WIDTH = 5 * HG_WIDTH + 3 * ATT_WIDTH
COL_HQ, COL_HFF, COL_HFB, COL_HI, COL_HG, COL_AQ, COL_AK, COL_AV = range(8)

VMEM_LIMIT_BYTES = 56 * 1024 * 1024

TM_IN = 1024
TM_OUT = 512
FF_CHUNK = 256
HG_CHUNK = 128
HG_SUB = 64
HG_BLOCK = 1024
ATT_TILE = 2048
ATT_HALO = HALF_WINDOW * max(DILATIONS)
ATT_GQ = 64
ATT_TRIP_QUERIES = 512
NEG = -1e30
LOG2E = 1.4426950408889634


def _cparams(sem):
    return pltpu.CompilerParams(dimension_semantics=sem, vmem_limit_bytes=VMEM_LIMIT_BYTES)


def _ada_kernel(c_ref, w_ref, b_ref, o_ref):
    c = c_ref[...]
    sc = c * (1.0 / (1.0 + jnp.exp(-c)))
    rows = sc.shape[0]
    s_hi = sc.astype(BF16)
    s_lo = (sc - s_hi.astype(F32)).astype(BF16)
    w = w_ref[...]
    w_hi = w.astype(BF16)
    w_lo = (w - w_hi.astype(F32)).astype(BF16)
    p = jnp.dot(jnp.concatenate([s_hi, s_lo], axis=0), w_hi, preferred_element_type=F32)
    q = jnp.dot(s_hi, w_lo, preferred_element_type=F32)
    o_ref[...] = p[:rows] + p[rows:] + q + b_ref[...]


def _ada(c_pad, w_ada, b_ada):
    rows, d = c_pad.shape
    n = w_ada.shape[1]
    bn = 1536
    return pl.pallas_call(
        _ada_kernel,
        grid=(n // bn,),
        in_specs=[pl.BlockSpec((rows, d), lambda j: (0, 0)),
                  pl.BlockSpec((d, bn), lambda j: (0, j)),
                  pl.BlockSpec((1, bn), lambda j: (0, j))],
        out_specs=pl.BlockSpec((rows, bn), lambda j: (0, j)),
        out_shape=jax.ShapeDtypeStruct((rows, n), F32),
        compiler_params=_cparams(("arbitrary",)),
        name="ada",
    )(c_pad, w_ada, b_ada)


def _t5_bucket_np(rel):
    half = NUM_BUCKETS // 2
    max_exact = half // 2
    ret = np.where(rel > 0, half, 0)
    n = np.abs(rel)
    nf = np.maximum(n, 1).astype(np.float64)
    large = max_exact + (np.log(nf / max_exact) / math.log(MAX_DISTANCE / max_exact)
                         * (half - max_exact)).astype(np.int64)
    large = np.minimum(large, half - 1)
    return (ret + np.where(n < max_exact, n, large)).astype(np.int32)


def _bucket_tiles(gq):
    gk = gq + 2 * HALF_WINDOW
    off = (np.arange(gk)[None, :] - HALF_WINDOW) - np.arange(gq)[:, None]
    tiles = []
    for d in DILATIONS:
        tiles.append(np.where(np.abs(off) <= HALF_WINDOW, _t5_bucket_np(off * d), -1))
    return np.stack(tiles).astype(np.int32)


def _bias_kernel(rb_ref, bk_ref, o_ref):
    bk = bk_ref[0]
    for h in range(ATT_HEADS):
        acc = jnp.full(bk.shape, NEG, F32)
        for b in range(NUM_BUCKETS):
            acc = jnp.where(bk == b, rb_ref[b, h] * LOG2E, acc)
        o_ref[0, h] = acc


def _bias_tiles(rel_bias, buckets):
    npat, gq, gk = buckets.shape
    return pl.pallas_call(
        _bias_kernel,
        grid=(npat,),
        in_specs=[pl.BlockSpec(memory_space=pltpu.SMEM),
                  pl.BlockSpec((1, gq, gk), lambda p: (p, 0, 0))],
        out_specs=pl.BlockSpec((1, ATT_HEADS, gq, gk), lambda p: (p, 0, 0, 0)),
        out_shape=jax.ShapeDtypeStruct((npat, ATT_HEADS, gq, gk), F32),
        compiler_params=_cparams(("arbitrary",)),
        name="bias_tiles",
    )(rel_bias, buckets)


def _inproj_kernel(x_ref, mod_ref, nw_ref, w_ref, z_ref):
    x = x_ref[0]
    ms = jnp.mean(x * x, axis=-1, keepdims=True)
    y = x * lax.rsqrt(ms + EPS) * nw_ref[...]
    shift = mod_ref[0, 0:1, :]
    scale = mod_ref[0, 1:2, :]
    h = (y * (1.0 + scale) + shift).astype(BF16)
    for j in range(IN_WIDTH // 512):
        cols = slice(j * 512, (j + 1) * 512)
        z_ref[0, :, cols] = jnp.dot(h, w_ref[:, cols], preferred_element_type=F32).astype(BF16)


def _inproj(x, mod, norm_w, w_in_bf16):
    b, s, d = x.shape
    return pl.pallas_call(
        _inproj_kernel,
        grid=(b, s // TM_IN),
        in_specs=[pl.BlockSpec((1, TM_IN, d), lambda bi, i: (bi, i, 0)),
                  pl.BlockSpec((1, 6, d), lambda bi, i: (bi, 0, 0)),
                  pl.BlockSpec((1, d), lambda bi, i: (0, 0)),
                  pl.BlockSpec((d, IN_WIDTH), lambda bi, i: (0, 0), pipeline_mode=pl.Buffered(1))],
        out_specs=pl.BlockSpec((1, TM_IN, IN_WIDTH), lambda bi, i: (bi, i, 0)),
        out_shape=jax.ShapeDtypeStruct((b, s, IN_WIDTH), BF16),
        compiler_params=_cparams(("arbitrary", "arbitrary")),
        name="inproj",
    )(x, mod, norm_w, w_in_bf16)


_NT =(((1,), (1,)), ((), ()))
_TN = (((0,), (0,)), ((), ()))


def _hgrn_gates(z, lb, log2_one_m_lb):
    e = jnp.exp(-jnp.abs(z))
    ope = 1.0 + e
    l1 = jnp.log2(ope)
    w = jnp.where(z >= 0, 1.0, e)
    logf = jnp.log2(lb * ope + (1.0 - lb) * w) - l1
    logk = log2_one_m_lb - jnp.maximum(z, 0.0) * LOG2E - l1
    return logf, logk


def _hgrn_decays(q, logf, logk, tri, reverse):
    L = HG_CHUNK
    hi = logf.astype(BF16)
    lo = (logf - hi.astype(F32)).astype(BF16)
    res = jnp.dot(tri, jnp.concatenate([hi, lo], axis=1), preferred_element_type=F32)
    g = res[:, :HG_WIDTH] + res[:, HG_WIDTH:]

    nsub = L // HG_SUB
    ref_off = HG_SUB // 2 if reverse else HG_SUB // 2 - 1
    grefs = [g[i * HG_SUB + ref_off:i * HG_SUB + ref_off + 1, :] for i in range(nsub)]
    gref_full = jnp.concatenate([jnp.broadcast_to(r, (HG_SUB, HG_WIDTH)) for r in grefs], axis=0)
    egref_full = jnp.concatenate([jnp.broadcast_to(jnp.exp2(r), (HG_SUB, HG_WIDTH)) for r in grefs],
                                 axis=0)
    qq32 = q * jnp.exp2(g - gref_full)
    qq = qq32.astype(BF16)
    qg = (qq32 * egref_full).astype(BF16)

    lkg = logk - g
    g_end = g[0:1, :] if reverse else g[L - 1:L, :]
    kk = []
    for i in range(nsub):
        lo_r, hi_r = (i * HG_SUB, L) if reverse else (0, (i + 1) * HG_SUB)
        part = jnp.exp2(lkg[lo_r:hi_r] + grefs[i])
        if hi_r - lo_r == L:
            kd = (part * jnp.exp2(g_end - grefs[i])).astype(BF16)
        pad = jnp.zeros((L - (hi_r - lo_r), HG_WIDTH), BF16)
        pieces = [pad, part.astype(BF16)] if reverse else [part.astype(BF16), pad]
        kk.append(jnp.concatenate(pieces, axis=0) if pad.shape[0] else part.astype(BF16))
    return qq, kk, qg, kd, jnp.exp2(g_end)


def _hgrn_kernel(qf_ref, ff_ref, vf_ref, qb_ref, fb_ref, vb_ref, lbp_ref, tri_ref,
                 of_ref, ob_ref, stf_ref, stb_ref):
    @pl.when(pl.program_id(1) == 0)
    def _():
        stf_ref[...] = jnp.zeros_like(stf_ref)
        stb_ref[...] = jnp.zeros_like(stb_ref)

    lbp = lbp_ref[...]
    lbe = jnp.exp(lbp - jnp.max(lbp, axis=0, keepdims=True))
    lb_all = lbe[0] / jnp.sum(lbe, axis=0)
    log2_one_m_lb = jnp.log2(1.0 - lb_all)

    L = HG_CHUNK
    nchunk = HG_BLOCK // L
    nsub = L // HG_SUB
    t_idx = lax.broadcasted_iota(jnp.int32, (L, L), 0)
    s_idx = lax.broadcasted_iota(jnp.int32, (L, L), 1)
    dirs = ((qf_ref, ff_ref, vf_ref, of_ref, stf_ref, False),
            (qb_ref, fb_ref, vb_ref, ob_ref, stb_ref, True))

    units = [(di, h) for h in range(HG_HEADS) for di in range(2)]

    def chunk_rows(ci):
        return (pl.ds(pl.multiple_of(ci * L, L), L),
                pl.ds(pl.multiple_of((nchunk - 1 - ci) * L, L), L))

    def body(ci, carry):
        rows = chunk_rows(ci)
        gates = [_hgrn_gates(f_ref[0, rows[di], :].astype(F32), lb_all[di:di + 1],
                             log2_one_m_lb[di:di + 1])
                 for di, (_, f_ref, _, _, _, _) in enumerate(dirs)]
        dec = [_hgrn_decays(q_ref[0, rows[di], :].astype(F32), gates[di][0], gates[di][1],
                            tri_ref[di], rev)
               for di, (q_ref, _, _, _, _, rev) in enumerate(dirs)]
        scores = {}
        for di, h in units:
            cols = slice(h * HG_KEY, (h + 1) * HG_KEY)
            qq, kk = dec[di][0], dec[di][1]
            scores[di, h] = jnp.concatenate(
                [lax.dot_general(qq[i * HG_SUB:(i + 1) * HG_SUB, cols], kk[i][:, cols], _NT,
                                 preferred_element_type=F32) for i in range(nsub)], axis=0)
        for di, h in units:
            cols = slice(h * HG_KEY, (h + 1) * HG_KEY)
            _, _, v_ref, o_ref, st_ref, rev = dirs[di]
            _, _, qg, kd, e_end = dec[di]
            causal = (s_idx >= t_idx) if rev else (s_idx <= t_idx)
            a = jnp.where(causal, scores[di, h], 0.0).astype(BF16)
            vh = v_ref[0, rows[di], cols]
            st = st_ref[h]
            o = (jnp.dot(a, vh, preferred_element_type=F32)
                 + lax.dot_general(qg[:, cols], st.astype(BF16), _NT, preferred_element_type=F32))
            o_ref[0, rows[di], cols] = o.astype(BF16)
            st_ref[h] = (st * e_end[:, cols]
                         + lax.dot_general(vh, kd[:, cols], _TN, preferred_element_type=F32))
        return carry

    lax.fori_loop(0, nchunk, body, 0)


def _hgrn(z, lb_params, tri):
    b, s, _ = z.shape
    nb = s // HG_BLOCK

    def fwd(col):
        return pl.BlockSpec((1, HG_BLOCK, HG_WIDTH), lambda bi, j: (bi, j, col))

    def bwd(col):
        return pl.BlockSpec((1, HG_BLOCK, HG_WIDTH), lambda bi, j: (bi, nb - 1 - j, col))

    out = jax.ShapeDtypeStruct((b, s, HG_WIDTH), BF16)
    return pl.pallas_call(
        _hgrn_kernel,
        grid=(b, nb),
        in_specs=[fwd(COL_HQ), fwd(COL_HFF), fwd(COL_HI),
                  bwd(COL_HQ), bwd(COL_HFB), bwd(COL_HI),
                  pl.BlockSpec(lb_params.shape, lambda bi, j: (0, 0, 0)),
                  pl.BlockSpec((2, HG_CHUNK, HG_CHUNK), lambda bi, j: (0, 0, 0))],
        out_specs=[pl.BlockSpec((1, HG_BLOCK, HG_WIDTH), lambda bi, j: (bi, j, 0)),
                   pl.BlockSpec((1, HG_BLOCK, HG_WIDTH), lambda bi, j: (bi, nb - 1 - j, 0))],
        out_shape=[out, out],
        scratch_shapes=[pltpu.VMEM((HG_HEADS, HG_KEY, HG_KEY), F32),
                        pltpu.VMEM((HG_HEADS, HG_KEY, HG_KEY), F32)],
        compiler_params=_cparams(("arbitrary", "arbitrary")),
        name="hgrn",
    )(z, z, z, z, z, z, lb_params, tri)


ATT_PAIRS = ATT_HEADS // 2


def _att_geometry(d):
    nq = ATT_TILE // d
    assert nq % ATT_GQ == 0
    return nq, nq // ATT_GQ, nq + 2 * HALF_WINDOW


ATT_SLAB = 256
ATT_PERM_ROWS = 256


def _perm_matrices():
    n = ATT_PERM_ROWS
    mats = []
    for d in DILATIONS[1:]:
        p = np.zeros((n, n), np.float32)
        i = np.arange(n)
        p[i, d * (i % (n // d)) + i // (n // d)] = 1.0
        mats.append(p)
    return jnp.asarray(np.stack(mats), dtype=BF16)


ATT_REGROUP_BATCH = 8


def _attn_regroup(jobs, perm_ref, lanes):
    for i in range(0, len(jobs), ATT_REGROUP_BATCH):
        batch = jobs[i:i + ATT_REGROUP_BATCH]
        ys = [jnp.dot(perm_ref[p - 1], src_ref[0, row:row + ATT_PERM_ROWS, lanes],
                      preferred_element_type=F32)
              for p, d, src_ref, row, _, _, _, _ in batch]
        for (p, d, _, _, dst_ref, m0, rows_per_res, scale), y in zip(batch, ys):
            bm = ATT_PERM_ROWS // d
            if scale is not None:
                y = y * scale
            y = y.astype(BF16)
            for r in range(d):
                dst_ref[r * rows_per_res + m0:r * rows_per_res + m0 + bm, :] = y[r * bm:(r + 1) * bm]


def _attn_kernel(q_ref, kp_ref, kc_ref, kn_ref, vp_ref, vc_ref, vn_ref, bias_ref, perm_ref, o_ref,
                 q1_ref, k1_ref, v1_ref, q4_ref, k4_ref, v4_ref, q16_ref, k16_ref, v16_ref,
                 sc_ref, *part_refs, seq_len):
    T = ATT_TILE
    gq, gk = ATT_GQ, ATT_GQ + 2 * HALF_WINDOW
    t0 = pl.program_id(1) * T
    q_res = (q1_ref, q4_ref, q16_ref)
    k_res = (k1_ref, k4_ref, k16_ref)
    v_res = (v1_ref, v4_ref, v16_ref)
    parts = [part_refs[3 * p:3 * p + 3] for p in range(len(DILATIONS))]
    lane = lax.broadcasted_iota(jnp.int32, (1, 128), 1)
    first = lane < ATT_HEAD_DIM
    jk = lax.broadcasted_iota(jnp.int32, (1, gk), 1)
    q_scale = ATT_HEAD_DIM ** -0.5 * LOG2E
    kv_srcs = ((k_res, (kp_ref, kc_ref, kn_ref)), (v_res, (vp_ref, vc_ref, vn_ref)))

    def slab_body(slab, carry):
        sl = pl.ds(pl.multiple_of(slab * ATT_SLAB, ATT_SLAB), ATT_SLAB)
        q1_ref[...] = (q_ref[0, :, sl].astype(F32) * q_scale).astype(BF16)
        for res, (prv, cur, nxt) in kv_srcs:
            res[0][0:HALF_WINDOW, :] = prv[0, ATT_HALO - HALF_WINDOW:ATT_HALO, sl]
            res[0][HALF_WINDOW:HALF_WINDOW + T, :] = cur[0, :, sl]
            res[0][HALF_WINDOW + T:, :] = nxt[0, 0:HALF_WINDOW, sl]
        jobs = []
        for p, d in enumerate(DILATIONS):
            if p == 0:
                continue
            nq, _, nk = _att_geometry(d)
            halo = d * HALF_WINDOW
            bm = ATT_PERM_ROWS // d

            def blocks(src, row0, ntok, dst, m0, rows_per_res, scale, p=p, d=d, bm=bm):
                return [(p, d, src, row0 + b * ATT_PERM_ROWS, dst, m0 + b * bm, rows_per_res, scale)
                        for b in range(ntok // ATT_PERM_ROWS)]

            jobs += blocks(q_ref, 0, T, q_res[p], 0, nq, q_scale)
            for res, (prv, cur, nxt) in kv_srcs:
                jobs += blocks(prv, ATT_HALO - halo, halo, res[p], 0, nk, None)
                jobs += blocks(cur, 0, T, res[p], HALF_WINDOW, nk, None)
                jobs += blocks(nxt, 0, halo, res[p], HALF_WINDOW + nq, nk, None)
        _attn_regroup(jobs, perm_ref, sl)

        def pair_body(j, carry):
            pair = slab * (ATT_SLAB // 128) + j
            lanes = pl.ds(pl.multiple_of(j * 128, 128), 128)
            trips = [_AttnTrip(p, d, ti, pair, lanes, t0, seq_len, first, jk, q_res[p], k_res[p],
                               v_res[p], bias_ref, parts[p])
                     for p, d in enumerate(DILATIONS) for ti in range(T // ATT_TRIP_QUERIES)]
            for sub in range(trips[0].n):
                sc_ref[0, sub] = trips[0].scores(sub)
            for k, trip in enumerate(trips):
                slot = k % 2
                soft = []
                for sub in range(trip.n):
                    if k + 1 < len(trips):
                        sc_ref[1 - slot, sub] = trips[k + 1].scores(sub)
                    soft.append(trip.softmax(sub, sc_ref[slot, sub]))
                for sub in range(trip.n):
                    trip.output(sub, *soft[sub])
            for p in range(len(DILATIONS) - 1, 0, -1):
                _attn_merge(parts[p - 1], parts[p], DILATIONS[p - 1], DILATIONS[p])
            o_ref[0, :, pl.ds(pl.multiple_of(pair * 128, 128), 128)] = \
                (parts[0][0][...] / parts[0][2][...]).astype(BF16)
            return carry

        lax.fori_loop(0, ATT_SLAB // 128, pair_body, 0)
        return carry

    lax.fori_loop(0, ATT_WIDTH // ATT_SLAB, slab_body, 0)


def _attn_merge(run, pat, d_run, d_pat):
    accn_ref, mn_ref, ln_ref = run
    accp_ref, mp_ref, lp_ref = pat
    nq = ATT_TILE // d_pat
    for r in range(d_pat):
        nat = (pl.ds((r % d_run) * (ATT_TILE // d_run) + r // d_run, nq, stride=d_pat // d_run),
               slice(None))
        res = (slice(r * nq, (r + 1) * nq), slice(None))
        m_old = mn_ref[nat]
        m_pat = mp_ref[res]
        m_new = jnp.maximum(m_old, m_pat)
        a_old = jnp.exp2(m_old - m_new)
        a_pat = jnp.exp2(m_pat - m_new)
        accn_ref[nat] = accn_ref[nat] * a_old + accp_ref[res] * a_pat
        ln_ref[nat] = ln_ref[nat] * a_old + lp_ref[res] * a_pat
        mn_ref[nat] = m_new


class _AttnTrip:
    def __init__(self, p, d, ti, pair, lanes, t0, seq_len, first, jk, q_ref, k_ref, v_ref, bias_ref,
                 dst):
        nq, gpr, nk = _att_geometry(d)
        self.gq, self.gk = ATT_GQ, ATT_GQ + 2 * HALF_WINDOW
        self.n = ATT_TRIP_QUERIES // ATT_GQ
        self.p, self.pair, self.lanes, self.first = p, pair, lanes, first
        self.q_ref, self.k_ref, self.v_ref, self.bias_ref, self.dst = q_ref, k_ref, v_ref, bias_ref, dst
        self.q0, self.k0, self.kvalid = [], [], []
        for sub in range(self.n):
            gi = ti * self.n + sub
            r, g = gi // gpr, gi % gpr
            self.q0.append(gi * ATT_GQ)
            self.k0.append(r * nk + g * ATT_GQ)
            ktok = t0 + r + d * (g * ATT_GQ - HALF_WINDOW + jk)
            self.kvalid.append((ktok >= 0) & (ktok < seq_len))

    def scores(self, sub):
        qg = self.q_ref[self.q0[sub]:self.q0[sub] + self.gq, self.lanes]
        kg = self.k_ref[self.k0[sub]:self.k0[sub] + self.gk, self.lanes]
        zero = jnp.zeros_like(qg)
        q2 = jnp.concatenate([jnp.where(self.first, qg, zero), jnp.where(self.first, zero, qg)],
                             axis=0)
        return lax.dot_general(q2, kg, _NT, preferred_element_type=F32)

    def softmax(self, sub, sc):
        b2 = self.bias_ref[self.p, self.pair]
        sc = jnp.where(self.kvalid[sub], sc + b2, NEG)
        mx = jnp.max(sc, axis=-1, keepdims=True)
        return jnp.exp2((sc - mx).astype(BF16)), mx

    def output(self, sub, pe, mx):
        gq = self.gq
        ls = jnp.sum(pe.astype(F32), axis=-1, keepdims=True)
        vg = self.v_ref[self.k0[sub]:self.k0[sub] + self.gk, self.lanes]
        pv = jnp.dot(pe, vg, preferred_element_type=F32)
        rows = (slice(self.q0[sub], self.q0[sub] + gq), slice(None))
        self.dst[0][rows] = jnp.where(self.first, pv[0:gq], pv[gq:2 * gq])
        self.dst[1][rows] = jnp.where(self.first, mx[0:gq], mx[gq:2 * gq])
        self.dst[2][rows] = jnp.where(self.first, ls[0:gq], ls[gq:2 * gq])


def _attn(z, bias):
    b, s, _ = z.shape
    perm = _perm_matrices()
    nt = s // ATT_TILE
    blk = (1, ATT_TILE, ATT_WIDTH)
    halo_blk = (1, ATT_HALO, ATT_WIDTH)
    hpt = ATT_TILE // ATT_HALO
    nh = s // ATT_HALO

    def cur(col):
        return pl.BlockSpec(blk, lambda bi, i: (bi, i, col))

    def prev(col):
        return pl.BlockSpec(halo_blk, lambda bi, i: (bi, jnp.maximum(i * hpt - 1, 0), col))

    def nxt(col):
        return pl.BlockSpec(halo_blk, lambda bi, i: (bi, jnp.minimum((i + 1) * hpt, nh - 1), col))

    return pl.pallas_call(
        functools.partial(_attn_kernel, seq_len=s),
        grid=(b, nt),
        in_specs=[cur(COL_AQ), prev(COL_AK), cur(COL_AK), nxt(COL_AK),
                  prev(COL_AV), cur(COL_AV), nxt(COL_AV),
                  pl.BlockSpec(bias.shape, lambda bi, i: (0, 0, 0, 0), pipeline_mode=pl.Buffered(1)),
                  pl.BlockSpec(perm.shape, lambda bi, i: (0, 0, 0), pipeline_mode=pl.Buffered(1))],
        out_specs=pl.BlockSpec(blk, lambda bi, i: (bi, i, 0)),
        out_shape=jax.ShapeDtypeStruct((b, s, ATT_WIDTH), BF16),
        scratch_shapes=(
            [pltpu.VMEM((rows, ATT_SLAB), BF16)
             for d in DILATIONS
             for rows in (ATT_TILE, d * _att_geometry(d)[2], d * _att_geometry(d)[2])]
            + [pltpu.VMEM((2, ATT_TRIP_QUERIES // ATT_GQ, 2 * ATT_GQ, ATT_GQ + 2 * HALF_WINDOW), F32)]
            + [pltpu.VMEM((ATT_TILE, 128), F32) for _ in range(3 * len(DILATIONS))]),
        compiler_params=_cparams(("arbitrary", "arbitrary")),
        name="attn",
    )(z, z, z, z, z, z, z, bias, perm)


def _silu(t):
    return t * (0.5 + 0.5 * jnp.tanh(0.5 * t))


def _out_kernel(x_ref, of_ref, ob_ref, g_ref, att_ref, mod_ref, gnw_ref, anw_ref, n2w_ref, fnw_ref,
                wo_ref, wg_ref, wu_ref, wd_ref, out_ref, h2_ref, acc_ref, act_ref):
    x = x_ref[0]
    o = of_ref[0].astype(F32) + ob_ref[0].astype(F32)
    gate = _silu(g_ref[0].astype(F32))
    parts = []
    for h in range(HG_HEADS):
        cols = slice(h * HG_KEY, (h + 1) * HG_KEY)
        oh = o[:, cols]
        ms = jnp.mean(oh * oh, axis=-1, keepdims=True)
        parts.append(oh * lax.rsqrt(ms + EPS) * gnw_ref[:, cols] * gate[:, cols])
    att = att_ref[0].astype(F32)
    ms = jnp.mean(att * att, axis=-1, keepdims=True)
    parts.append(att * lax.rsqrt(ms + EPS) * anw_ref[...])
    y = jnp.concatenate(parts, axis=-1).astype(BF16)
    mix = jnp.dot(y, wo_ref[...], preferred_element_type=F32)
    x1 = x + mod_ref[0, 2:3, :] * mix

    ms = jnp.mean(x1 * x1, axis=-1, keepdims=True)
    h2 = x1 * lax.rsqrt(ms + EPS) * n2w_ref[...]
    h2_ref[...] = (h2 * (1.0 + mod_ref[0, 4:5, :]) + mod_ref[0, 3:4, :]).astype(BF16)

    nchunks = D_FF // FF_CHUNK

    def act_chunk(j):
        c0 = pl.multiple_of(j * FF_CHUNK, FF_CHUNK)
        h2b = h2_ref[...]
        gt = jnp.dot(h2b, wg_ref[:, pl.ds(c0, FF_CHUNK)], preferred_element_type=F32)
        up = jnp.dot(h2b, wu_ref[:, pl.ds(c0, FF_CHUNK)], preferred_element_type=F32)
        return gt, up

    def down(j):
        c0 = pl.multiple_of(j * FF_CHUNK, FF_CHUNK)
        return jnp.dot(act_ref[j % 2], wd_ref[pl.ds(c0, FF_CHUNK), :], preferred_element_type=F32)

    gt, up = act_chunk(0)
    act_ref[0] = (_silu(gt) * up).astype(BF16)
    acc_ref[...] = jnp.zeros_like(acc_ref)

    def ff_body(j, carry):
        gt, up = act_chunk(j + 1)
        acc_ref[...] += down(j)
        act_ref[(j + 1) % 2] = (_silu(gt) * up).astype(BF16)
        return carry

    lax.fori_loop(0, nchunks - 1, ff_body, 0, unroll=True)
    x2 = x1 + mod_ref[0, 5:6, :] * (acc_ref[...] + down(nchunks - 1))
    ms = jnp.mean(x2 * x2, axis=-1, keepdims=True)
    out_ref[0] = x2 * lax.rsqrt(ms + EPS) * fnw_ref[...]


def _out(x, o_f, o_b, z, att, mod, gn_w, an_w, n2_w, fn_w, wo, wg, wu, wd):
    b, s, d = x.shape

    def tok(width, col=0):
        return pl.BlockSpec((1, TM_OUT, width), lambda bi, i: (bi, i, col))

    def const(shape):
        return pl.BlockSpec(shape, lambda bi, i: (0,) * len(shape), pipeline_mode=pl.Buffered(1))

    return pl.pallas_call(
        _out_kernel,
        grid=(b, s // TM_OUT),
        in_specs=[tok(d), tok(HG_WIDTH), tok(HG_WIDTH), tok(HG_WIDTH, COL_HG), tok(ATT_WIDTH),
                  pl.BlockSpec((1, 6, d), lambda bi, i: (bi, 0, 0)),
                  const((1, HG_WIDTH)), const((1, ATT_WIDTH)), const((1, d)), const((1, d)),
                  const(wo.shape), const(wg.shape), const(wu.shape), const(wd.shape)],
        out_specs=tok(d),
        out_shape=jax.ShapeDtypeStruct((b, s, d), F32),
        scratch_shapes=[pltpu.VMEM((TM_OUT, d), BF16), pltpu.VMEM((TM_OUT, d), F32),
                        pltpu.VMEM((2, TM_OUT, FF_CHUNK), BF16)],
        compiler_params=_cparams(("arbitrary", "arbitrary")),
        name="out_ffn",
    )(x, o_f, o_b, z, att, mod, gn_w, an_w, n2_w, fn_w, wo, wg, wu, wd)


def _cumsum_matrices():
    t = np.arange(HG_CHUNK)
    fwd = (t[None, :] <= t[:, None]).astype(np.float32)
    bwd = (t[None, :] >= t[:, None]).astype(np.float32)
    return jnp.asarray(np.stack([fwd, bwd]), dtype=BF16)


def kernel(x, c, rel_bias, w_ada, b_ada, norm1_w, w_in, hg_lower_bound, hg_norm_w, attn_norm_w,
           w_out, norm2_w, w_gate, w_up, w_down, final_norm_w):
    b, s, d = x.shape
    assert d == D_MODEL and s % ATT_TILE == 0 and s % HG_BLOCK == 0 and w_ada.shape[0] == 1
    c_pad = jnp.zeros((16, d), F32).at[:b].set(c)
    mod = _ada(c_pad, w_ada[0], b_ada[0][None, :])[:b].reshape(b, 6, d)

    bias = _bias_tiles(rel_bias.astype(F32), jnp.asarray(_bucket_tiles(ATT_GQ)))
    bias = bias.reshape(len(DILATIONS), ATT_PAIRS, 2 * ATT_GQ, ATT_GQ + 2 * HALF_WINDOW)
    z = _inproj(x, mod, norm1_w[0][None, :], w_in[0].astype(BF16))
    o_f, o_b = _hgrn(z, hg_lower_bound.astype(F32), _cumsum_matrices())
    att = _attn(z, bias)
    return _out(x, o_f, o_b, z, att, mod,
                hg_norm_w[0][None, :], attn_norm_w[0][None, :], norm2_w[0][None, :],
                final_norm_w[None, :],
                w_out[0].astype(BF16), w_gate[0].astype(BF16), w_up[0].astype(BF16),
                w_down[0].astype(BF16))
```

```python
import functools
import math

import numpy as np
import jax
import jax.numpy as jnp
from jax import lax
from jax.experimental import pallas as pl
from jax.experimental.pallas import tpu as pltpu

F32 = jnp.float32
BF16 = jnp.bfloat16

D_MODEL = 1024
HG_HEADS = 4
HG_KEY = 128
HG_WIDTH = HG_HEADS * HG_KEY
ATT_HEADS = 8
ATT_HEAD_DIM = 64
ATT_WIDTH = ATT_HEADS * ATT_HEAD_DIM
DILATIONS = (1, 4, 16)
HALF_WINDOW = 64
NUM_BUCKETS = 32
MAX_DISTANCE = 1024
D_FF = 2816
EPS = 1e-6
IN_WIDTH = 5 * HG_WIDTH + 3 * ATT_WIDTH
COL_HQ, COL_HFF, COL_HFB, COL_HI, COL_HG, COL_AQ, COL_AK, COL_AV = range(8)

VMEM_LIMIT_BYTES = 56 * 1024 * 1024

TM_IN = 1024
TM_OUT = 512
OUT_STREAMS = 2
FF_CHUNK = 256
HG_CHUNK = 128
HG_SUB = 64
HG_BLOCK = 1024
ATT_TILE = 2048
ATT_HALO = HALF_WINDOW * max(DILATIONS)
ATT_GQ = 128
ATT_TRIP_QUERIES = 1024
NEG = -1e30
LOG2E = 1.4426950408889634


def _cparams(sem):
    return pltpu.CompilerParams(dimension_semantics=sem, vmem_limit_bytes=VMEM_LIMIT_BYTES)


def _ada_kernel(c_ref, w_ref, b_ref, o_ref):
    c = c_ref[...]
    sc = c * (1.0 / (1.0 + jnp.exp(-c)))
    rows = sc.shape[0]
    s_hi = sc.astype(BF16)
    s_lo = (sc - s_hi.astype(F32)).astype(BF16)
    w = w_ref[...]
    w_hi = w.astype(BF16)
    w_lo = (w - w_hi.astype(F32)).astype(BF16)
    p = jnp.dot(jnp.concatenate([s_hi, s_lo], axis=0), w_hi, preferred_element_type=F32)
    q = jnp.dot(s_hi, w_lo, preferred_element_type=F32)
    o_ref[...] = p[:rows] + p[rows:] + q + b_ref[...]


def _ada(c_pad, w_ada, b_ada):
    rows, d = c_pad.shape
    n = w_ada.shape[1]
    bn = 1536
    return pl.pallas_call(
        _ada_kernel,
        grid=(n // bn,),
        in_specs=[pl.BlockSpec((rows, d), lambda j: (0, 0)),
                  pl.BlockSpec((d, bn), lambda j: (0, j)),
                  pl.BlockSpec((1, bn), lambda j: (0, j))],
        out_specs=pl.BlockSpec((rows, bn), lambda j: (0, j)),
        out_shape=jax.ShapeDtypeStruct((rows, n), F32),
        compiler_params=_cparams(("arbitrary",)),
        name="ada",
    )(c_pad, w_ada, b_ada)


def _t5_bucket_np(rel):
    half = NUM_BUCKETS // 2
    max_exact = half // 2
    ret = np.where(rel > 0, half, 0)
    n = np.abs(rel)
    nf = np.maximum(n, 1).astype(np.float64)
    large = max_exact + (np.log(nf / max_exact) / math.log(MAX_DISTANCE / max_exact)
                         * (half - max_exact)).astype(np.int64)
    large = np.minimum(large, half - 1)
    return (ret + np.where(n < max_exact, n, large)).astype(np.int32)


def _bucket_tiles(gq):
    gk = gq + 2 * HALF_WINDOW
    off = (np.arange(gk)[None, :] - HALF_WINDOW) - np.arange(gq)[:, None]
    tiles = []
    for d in DILATIONS:
        tiles.append(np.where(np.abs(off) <= HALF_WINDOW, _t5_bucket_np(off * d), -1))
    return np.stack(tiles).astype(np.int32)


def _bias_kernel(rb_ref, bk_ref, o_ref):
    bk = bk_ref[0]
    for h in range(ATT_HEADS):
        acc = jnp.full(bk.shape, NEG, F32)
        for b in range(NUM_BUCKETS):
            acc = jnp.where(bk == b, rb_ref[b, h] * LOG2E, acc)
        o_ref[0, h] = acc


def _bias_tiles(rel_bias, buckets):
    npat, gq, gk = buckets.shape
    return pl.pallas_call(
        _bias_kernel,
        grid=(npat,),
        in_specs=[pl.BlockSpec(memory_space=pltpu.SMEM),
                  pl.BlockSpec((1, gq, gk), lambda p: (p, 0, 0))],
        out_specs=pl.BlockSpec((1, ATT_HEADS, gq, gk), lambda p: (p, 0, 0, 0)),
        out_shape=jax.ShapeDtypeStruct((npat, ATT_HEADS, gq, gk), F32),
        compiler_params=_cparams(("arbitrary",)),
        name="bias_tiles",
    )(rel_bias, buckets)


def _inproj_kernel(x_ref, mod_ref, nw_ref, w_ref, z_ref):
    x = x_ref[0]
    ms = jnp.mean(x * x, axis=-1, keepdims=True)
    y = x * lax.rsqrt(ms + EPS) * nw_ref[...]
    shift = mod_ref[0, 0:1, :]
    scale = mod_ref[0, 1:2, :]
    h = (y * (1.0 + scale) + shift).astype(BF16)
    for j in range(IN_WIDTH // 512):
        cols = slice(j * 512, (j + 1) * 512)
        z_ref[0, :, cols] = jnp.dot(h, w_ref[:, cols], preferred_element_type=F32).astype(BF16)


def _inproj(x, mod, norm_w, w_in_bf16):
    b, s, d = x.shape
    return pl.pallas_call(
        _inproj_kernel,
        grid=(b, s // TM_IN),
        in_specs=[pl.BlockSpec((1, TM_IN, d), lambda bi, i: (bi, i, 0)),
                  pl.BlockSpec((1, 6, d), lambda bi, i: (bi, 0, 0)),
                  pl.BlockSpec((1, d), lambda bi, i: (0, 0)),
                  pl.BlockSpec((d, IN_WIDTH), lambda bi, i: (0, 0), pipeline_mode=pl.Buffered(1))],
        out_specs=pl.BlockSpec((1, TM_IN, IN_WIDTH), lambda bi, i: (bi, i, 0)),
        out_shape=jax.ShapeDtypeStruct((b, s, IN_WIDTH), BF16),
        compiler_params=_cparams(("arbitrary", "arbitrary")),
        name="inproj",
    )(x, mod, norm_w, w_in_bf16)


_NT =(((1,), (1,)), ((), ()))
_TN = (((0,), (0,)), ((), ()))


def _hgrn_gates(z, lb, log2_one_m_lb):
    e = jnp.exp(-jnp.abs(z))
    ope = 1.0 + e
    l1 = jnp.log2(ope)
    w = jnp.where(z >= 0, 1.0, e)
    logf = jnp.log2(lb * ope + (1.0 - lb) * w) - l1
    logk = log2_one_m_lb - jnp.maximum(z, 0.0) * LOG2E - l1
    return logf, logk


def _hgrn_decays(q, logf, logk, tri, reverse):
    L = HG_CHUNK
    hi = logf.astype(BF16)
    lo = (logf - hi.astype(F32)).astype(BF16)
    res = jnp.dot(tri, jnp.concatenate([hi, lo], axis=1), preferred_element_type=F32)
    g = res[:, :HG_WIDTH] + res[:, HG_WIDTH:]

    nsub = L // HG_SUB
    ref_off = HG_SUB // 2 if reverse else HG_SUB // 2 - 1
    grefs = [g[i * HG_SUB + ref_off:i * HG_SUB + ref_off + 1, :] for i in range(nsub)]
    gref_full = jnp.concatenate([jnp.broadcast_to(r, (HG_SUB, HG_WIDTH)) for r in grefs], axis=0)
    egref_full = jnp.concatenate([jnp.broadcast_to(jnp.exp2(r), (HG_SUB, HG_WIDTH)) for r in grefs],
                                 axis=0)
    qq32 = q * jnp.exp2(g - gref_full)
    qq = qq32.astype(BF16)
    qg = (qq32 * egref_full).astype(BF16)

    lkg = logk - g
    g_end = g[0:1, :] if reverse else g[L - 1:L, :]
    kk = []
    for i in range(nsub):
        lo_r, hi_r = (i * HG_SUB, L) if reverse else (0, (i + 1) * HG_SUB)
        part = jnp.exp2(lkg[lo_r:hi_r] + grefs[i])
        if hi_r - lo_r == L:
            kd = (part * jnp.exp2(g_end - grefs[i])).astype(BF16)
        pad = jnp.zeros((L - (hi_r - lo_r), HG_WIDTH), BF16)
        pieces = [pad, part.astype(BF16)] if reverse else [part.astype(BF16), pad]
        kk.append(jnp.concatenate(pieces, axis=0) if pad.shape[0] else part.astype(BF16))
    return qq, kk, qg, kd, jnp.exp2(g_end)


def _hgrn_kernel(qf_ref, ff_ref, vf_ref, qb_ref, fb_ref, vb_ref, lbp_ref, tri_ref,
                 of_ref, ob_ref, stf_ref, stb_ref):
    @pl.when(pl.program_id(1) == 0)
    def _():
        stf_ref[...] = jnp.zeros_like(stf_ref)
        stb_ref[...] = jnp.zeros_like(stb_ref)

    lbp = lbp_ref[...]
    lbe = jnp.exp(lbp - jnp.max(lbp, axis=0, keepdims=True))
    lb_all = lbe[0] / jnp.sum(lbe, axis=0)
    log2_one_m_lb = jnp.log2(1.0 - lb_all)

    L = HG_CHUNK
    nchunk = HG_BLOCK // L
    nsub = L // HG_SUB
    t_idx = lax.broadcasted_iota(jnp.int32, (L, L), 0)
    s_idx = lax.broadcasted_iota(jnp.int32, (L, L), 1)
    dirs = ((qf_ref, ff_ref, vf_ref, of_ref, stf_ref, False),
            (qb_ref, fb_ref, vb_ref, ob_ref, stb_ref, True))

    units = [(di, h) for h in range(HG_HEADS) for di in range(2)]

    def chunk_rows(ci):
        return (pl.ds(pl.multiple_of(ci * L, L), L),
                pl.ds(pl.multiple_of((nchunk - 1 - ci) * L, L), L))

    def body(ci, carry):
        rows = chunk_rows(ci)
        gates = [_hgrn_gates(f_ref[0, rows[di], :].astype(F32), lb_all[di:di + 1],
                             log2_one_m_lb[di:di + 1])
                 for di, (_, f_ref, _, _, _, _) in enumerate(dirs)]
        dec = [_hgrn_decays(q_ref[0, rows[di], :].astype(F32), gates[di][0], gates[di][1],
                            tri_ref[di], rev)
               for di, (q_ref, _, _, _, _, rev) in enumerate(dirs)]
        scores = {}
        for di, h in units:
            cols = slice(h * HG_KEY, (h + 1) * HG_KEY)
            qq, kk = dec[di][0], dec[di][1]
            scores[di, h] = jnp.concatenate(
                [lax.dot_general(qq[i * HG_SUB:(i + 1) * HG_SUB, cols], kk[i][:, cols], _NT,
                                 preferred_element_type=F32) for i in range(nsub)], axis=0)
        for di, h in units:
            cols = slice(h * HG_KEY, (h + 1) * HG_KEY)
            _, _, v_ref, o_ref, st_ref, rev = dirs[di]
            _, _, qg, kd, e_end = dec[di]
            causal = (s_idx >= t_idx) if rev else (s_idx <= t_idx)
            a = jnp.where(causal, scores[di, h], 0.0).astype(BF16)
            vh = v_ref[0, rows[di], cols]
            st = st_ref[h]
            o = (jnp.dot(a, vh, preferred_element_type=F32)
                 + lax.dot_general(qg[:, cols], st.astype(BF16), _NT, preferred_element_type=F32))
            o_ref[0, rows[di], cols] = o.astype(BF16)
            st_ref[h] = (st * e_end[:, cols]
                         + lax.dot_general(vh, kd[:, cols], _TN, preferred_element_type=F32))
        return carry

    lax.fori_loop(0, nchunk, body, 0)


def _hgrn(z, lb_params, tri):
    b, s, _ = z.shape
    nb = s // HG_BLOCK

    def fwd(col):
        return pl.BlockSpec((1, HG_BLOCK, HG_WIDTH), lambda bi, j: (bi, j, col))

    def bwd(col):
        return pl.BlockSpec((1, HG_BLOCK, HG_WIDTH), lambda bi, j: (bi, nb - 1 - j, col))

    out = jax.ShapeDtypeStruct((b, s, HG_WIDTH), BF16)
    return pl.pallas_call(
        _hgrn_kernel,
        grid=(b, nb),
        in_specs=[fwd(COL_HQ), fwd(COL_HFF), fwd(COL_HI),
                  bwd(COL_HQ), bwd(COL_HFB), bwd(COL_HI),
                  pl.BlockSpec(lb_params.shape, lambda bi, j: (0, 0, 0)),
                  pl.BlockSpec((2, HG_CHUNK, HG_CHUNK), lambda bi, j: (0, 0, 0))],
        out_specs=[pl.BlockSpec((1, HG_BLOCK, HG_WIDTH), lambda bi, j: (bi, j, 0)),
                   pl.BlockSpec((1, HG_BLOCK, HG_WIDTH), lambda bi, j: (bi, nb - 1 - j, 0))],
        out_shape=[out, out],
        scratch_shapes=[pltpu.VMEM((HG_HEADS, HG_KEY, HG_KEY), F32),
                        pltpu.VMEM((HG_HEADS, HG_KEY, HG_KEY), F32)],
        compiler_params=_cparams(("arbitrary", "arbitrary")),
        name="hgrn",
    )(z, z, z, z, z, z, lb_params, tri)


ATT_PAIRS = ATT_HEADS // 2


def _att_geometry(d):
    nq = ATT_TILE // d
    assert nq % ATT_GQ == 0
    return nq, nq // ATT_GQ, nq + 2 * HALF_WINDOW


ATT_SLAB = 256
ATT_PERM_ROWS = 256


def _perm_matrices():
    n = ATT_PERM_ROWS
    mats = []
    for d in DILATIONS[1:]:
        p = np.zeros((n, n), np.float32)
        i = np.arange(n)
        p[i, d * (i % (n // d)) + i // (n // d)] = 1.0
        mats.append(p)
    return jnp.asarray(np.stack(mats), dtype=BF16)


ATT_REGROUP_BATCH = 8


def _attn_regroup(jobs, perm_ref, lanes):
    for i in range(0, len(jobs), ATT_REGROUP_BATCH):
        batch = jobs[i:i + ATT_REGROUP_BATCH]
        ys = [jnp.dot(perm_ref[p - 1], src_ref[0, row:row + ATT_PERM_ROWS, lanes],
                      preferred_element_type=F32)
              for p, d, src_ref, row, _, _, _, _ in batch]
        for (p, d, _, _, dst_ref, m0, rows_per_res, scale), y in zip(batch, ys):
            bm = ATT_PERM_ROWS // d
            if scale is not None:
                y = y * scale
            y = y.astype(BF16)
            for r in range(d):
                dst_ref[r * rows_per_res + m0:r * rows_per_res + m0 + bm, :] = y[r * bm:(r + 1) * bm]


def _attn_kernel(q_ref, kp_ref, kc_ref, kn_ref, vp_ref, vc_ref, vn_ref, bias_ref, perm_ref, o_ref,
                 q1_ref, k1_ref, v1_ref, q4_ref, k4_ref, v4_ref, q16_ref, k16_ref, v16_ref,
                 sc_ref, *part_refs, seq_len):
    T = ATT_TILE
    gq, gk = ATT_GQ, ATT_GQ + 2 * HALF_WINDOW
    t0 = pl.program_id(1) * T
    q_res = (q1_ref, q4_ref, q16_ref)
    k_res = (k1_ref, k4_ref, k16_ref)
    v_res = (v1_ref, v4_ref, v16_ref)
    parts = [part_refs[3 * p:3 * p + 3] for p in range(len(DILATIONS))]
    lane = lax.broadcasted_iota(jnp.int32, (1, 128), 1)
    first = lane < ATT_HEAD_DIM
    jk = lax.broadcasted_iota(jnp.int32, (1, gk), 1)
    q_scale = ATT_HEAD_DIM ** -0.5 * LOG2E
    kv_srcs = ((k_res, (kp_ref, kc_ref, kn_ref)), (v_res, (vp_ref, vc_ref, vn_ref)))

    def slab_body(slab, carry):
        sl = pl.ds(pl.multiple_of(slab * ATT_SLAB, ATT_SLAB), ATT_SLAB)
        q1_ref[...] = (q_ref[0, :, sl].astype(F32) * q_scale).astype(BF16)
        for res, (prv, cur, nxt) in kv_srcs:
            res[0][0:HALF_WINDOW, :] = prv[0, ATT_HALO - HALF_WINDOW:ATT_HALO, sl]
            res[0][HALF_WINDOW:HALF_WINDOW + T, :] = cur[0, :, sl]
            res[0][HALF_WINDOW + T:, :] = nxt[0, 0:HALF_WINDOW, sl]
        jobs = []
        for p, d in enumerate(DILATIONS):
            if p == 0:
                continue
            nq, _, nk = _att_geometry(d)
            halo = d * HALF_WINDOW
            bm = ATT_PERM_ROWS // d

            def blocks(src, row0, ntok, dst, m0, rows_per_res, scale, p=p, d=d, bm=bm):
                return [(p, d, src, row0 + b * ATT_PERM_ROWS, dst, m0 + b * bm, rows_per_res, scale)
                        for b in range(ntok // ATT_PERM_ROWS)]

            jobs += blocks(q_ref, 0, T, q_res[p], 0, nq, q_scale)
            for res, (prv, cur, nxt) in kv_srcs:
                jobs += blocks(prv, ATT_HALO - halo, halo, res[p], 0, nk, None)
                jobs += blocks(cur, 0, T, res[p], HALF_WINDOW, nk, None)
                jobs += blocks(nxt, 0, halo, res[p], HALF_WINDOW + nq, nk, None)
        _attn_regroup(jobs, perm_ref, sl)

        def pair_body(j, carry):
            pair = slab * (ATT_SLAB // 128) + j
            lanes = pl.ds(pl.multiple_of(j * 128, 128), 128)
            trips = [_AttnTrip(p, d, ti, pair, lanes, t0, seq_len, first, jk, q_res[p], k_res[p],
                               v_res[p], bias_ref, parts[p])
                     for p, d in enumerate(DILATIONS) for ti in range(T // ATT_TRIP_QUERIES)]
            for sub in range(trips[0].n):
                sc_ref[0, sub] = trips[0].scores(sub)
            for k, trip in enumerate(trips):
                slot = k % 2
                soft = []
                for sub in range(trip.n):
                    if k + 1 < len(trips):
                        sc_ref[1 - slot, sub] = trips[k + 1].scores(sub)
                    soft.append(trip.softmax(sub, sc_ref[slot, sub]))
                for sub in range(trip.n):
                    trip.output(sub, *soft[sub])
            for p in range(len(DILATIONS) - 1, 0, -1):
                _attn_merge(parts[p - 1], parts[p], DILATIONS[p - 1], DILATIONS[p])
            o_ref[0, :, pl.ds(pl.multiple_of(pair * 128, 128), 128)] = \
                (parts[0][0][...] / parts[0][2][...]).astype(BF16)
            return carry

        lax.fori_loop(0, ATT_SLAB // 128, pair_body, 0)
        return carry

    lax.fori_loop(0, ATT_WIDTH // ATT_SLAB, slab_body, 0)


def _attn_merge(run, pat, d_run, d_pat):
    accn_ref, mn_ref, ln_ref = run
    accp_ref, mp_ref, lp_ref = pat
    nq = ATT_TILE // d_pat
    for r in range(d_pat):
        nat = (pl.ds((r % d_run) * (ATT_TILE // d_run) + r // d_run, nq, stride=d_pat // d_run),
               slice(None))
        res = (slice(r * nq, (r + 1) * nq), slice(None))
        m_old = mn_ref[nat]
        m_pat = mp_ref[res]
        m_new = jnp.maximum(m_old, m_pat)
        a_old = jnp.exp2(m_old - m_new)
        a_pat = jnp.exp2(m_pat - m_new)
        accn_ref[nat] = accn_ref[nat] * a_old + accp_ref[res] * a_pat
        ln_ref[nat] = ln_ref[nat] * a_old + lp_ref[res] * a_pat
        mn_ref[nat] = m_new


class _AttnTrip:
    def __init__(self, p, d, ti, pair, lanes, t0, seq_len, first, jk, q_ref, k_ref, v_ref, bias_ref,
                 dst):
        nq, gpr, nk = _att_geometry(d)
        self.gq, self.gk = ATT_GQ, ATT_GQ + 2 * HALF_WINDOW
        self.n = ATT_TRIP_QUERIES // ATT_GQ
        self.p, self.pair, self.lanes, self.first = p, pair, lanes, first
        self.q_ref, self.k_ref, self.v_ref, self.bias_ref, self.dst = q_ref, k_ref, v_ref, bias_ref, dst
        self.q0, self.k0, self.kvalid = [], [], []
        for sub in range(self.n):
            gi = ti * self.n + sub
            r, g = gi // gpr, gi % gpr
            self.q0.append(gi * ATT_GQ)
            self.k0.append(r * nk + g * ATT_GQ)
            ktok = t0 + r + d * (g * ATT_GQ - HALF_WINDOW + jk)
            self.kvalid.append((ktok >= 0) & (ktok < seq_len))

    def scores(self, sub):
        qg = self.q_ref[self.q0[sub]:self.q0[sub] + self.gq, self.lanes]
        kg = self.k_ref[self.k0[sub]:self.k0[sub] + self.gk, self.lanes]
        zero = jnp.zeros_like(qg)
        q2 = jnp.concatenate([jnp.where(self.first, qg, zero), jnp.where(self.first, zero, qg)],
                             axis=0)
        return lax.dot_general(q2, kg, _NT, preferred_element_type=F32)

    def softmax(self, sub, sc):
        b2 = self.bias_ref[self.p, self.pair]
        sc = jnp.where(self.kvalid[sub], sc + b2, NEG)
        mx = jnp.max(sc, axis=-1, keepdims=True)
        return jnp.exp2(sc - mx), mx

    def output(self, sub, pe, mx):
        gq = self.gq
        ls = jnp.sum(pe, axis=-1, keepdims=True)
        vg = self.v_ref[self.k0[sub]:self.k0[sub] + self.gk, self.lanes]
        pv = jnp.dot(pe.astype(BF16), vg, preferred_element_type=F32)
        rows = (slice(self.q0[sub], self.q0[sub] + gq), slice(None))
        self.dst[0][rows] = jnp.where(self.first, pv[0:gq], pv[gq:2 * gq])
        self.dst[1][rows] = jnp.where(self.first, mx[0:gq], mx[gq:2 * gq])
        self.dst[2][rows] = jnp.where(self.first, ls[0:gq], ls[gq:2 * gq])


def _attn(z, bias):
    b, s, _ = z.shape
    perm = _perm_matrices()
    nt = s // ATT_TILE
    blk = (1, ATT_TILE, ATT_WIDTH)
    halo_blk = (1, ATT_HALO, ATT_WIDTH)
    hpt = ATT_TILE // ATT_HALO
    nh = s // ATT_HALO

    def cur(col):
        return pl.BlockSpec(blk, lambda bi, i: (bi, i, col))

    def prev(col):
        return pl.BlockSpec(halo_blk, lambda bi, i: (bi, jnp.maximum(i * hpt - 1, 0), col))

    def nxt(col):
        return pl.BlockSpec(halo_blk, lambda bi, i: (bi, jnp.minimum((i + 1) * hpt, nh - 1), col))

    return pl.pallas_call(
        functools.partial(_attn_kernel, seq_len=s),
        grid=(b, nt),
        in_specs=[cur(COL_AQ), prev(COL_AK), cur(COL_AK), nxt(COL_AK),
                  prev(COL_AV), cur(COL_AV), nxt(COL_AV),
                  pl.BlockSpec(bias.shape, lambda bi, i: (0, 0, 0, 0), pipeline_mode=pl.Buffered(1)),
                  pl.BlockSpec(perm.shape, lambda bi, i: (0, 0, 0), pipeline_mode=pl.Buffered(1))],
        out_specs=pl.BlockSpec(blk, lambda bi, i: (bi, i, 0)),
        out_shape=jax.ShapeDtypeStruct((b, s, ATT_WIDTH), BF16),
        scratch_shapes=(
            [pltpu.VMEM((rows, ATT_SLAB), BF16)
             for d in DILATIONS
             for rows in (ATT_TILE, d * _att_geometry(d)[2], d * _att_geometry(d)[2])]
            + [pltpu.VMEM((2, ATT_TRIP_QUERIES // ATT_GQ, 2 * ATT_GQ, ATT_GQ + 2 * HALF_WINDOW), F32)]
            + [pltpu.VMEM((ATT_TILE, 128), F32) for _ in range(3 * len(DILATIONS))]),
        compiler_params=_cparams(("arbitrary", "arbitrary")),
        name="attn",
    )(z, z, z, z, z, z, z, bias, perm)


def _silu(t):
    return t * (0.5 + 0.5 * jnp.tanh(0.5 * t))


def _out_kernel(x_ref, of_ref, ob_ref, g_ref, att_ref, mod_ref, gnw_ref, anw_ref, n2w_ref, fnw_ref,
                wo_ref, wg_ref, wu_ref, wd_ref, out_ref, x1_ref, h2_ref, act_ref):
    nchunks = D_FF // FF_CHUNK
    rows_per = TM_OUT // OUT_STREAMS

    def rows_of(s):
        return slice(s * rows_per, (s + 1) * rows_per)

    def mix_inputs(s):
        rows = rows_of(s)
        o = of_ref[0, rows, :].astype(F32) + ob_ref[0, rows, :].astype(F32)
        gate = _silu(g_ref[0, rows, :].astype(F32))
        parts = []
        for h in range(HG_HEADS):
            cols = slice(h * HG_KEY, (h + 1) * HG_KEY)
            oh = o[:, cols]
            ms = jnp.mean(oh * oh, axis=-1, keepdims=True)
            parts.append(oh * lax.rsqrt(ms + EPS) * gnw_ref[:, cols] * gate[:, cols])
        att = att_ref[0, rows, :].astype(F32)
        ms = jnp.mean(att * att, axis=-1, keepdims=True)
        parts.append(att * lax.rsqrt(ms + EPS) * anw_ref[...])
        return jnp.concatenate(parts, axis=-1).astype(BF16)

    def token_mix(s, y):
        mix = jnp.dot(y, wo_ref[...], preferred_element_type=F32)
        x1 = x_ref[0, rows_of(s), :] + mod_ref[0, 2:3, :] * mix
        x1_ref[s] = x1
        ms = jnp.mean(x1 * x1, axis=-1, keepdims=True)
        h2 = x1 * lax.rsqrt(ms + EPS) * n2w_ref[...]
        h2_ref[s] = (h2 * (1.0 + mod_ref[0, 4:5, :]) + mod_ref[0, 3:4, :]).astype(BF16)

    def gate_up(s, j):
        cols = slice(j * FF_CHUNK, (j + 1) * FF_CHUNK)
        h2b = h2_ref[s]
        gt = jnp.dot(h2b, wg_ref[:, cols], preferred_element_type=F32)
        up = jnp.dot(h2b, wu_ref[:, cols], preferred_element_type=F32)
        return gt, up

    def activations(s, fillers):
        for j in range(nchunks):
            gt, up = gate_up(s, j)
            act_ref[s, :, j * FF_CHUNK:(j + 1) * FF_CHUNK] = (_silu(gt) * up).astype(BF16)
            if j in fillers:
                fillers[j]()

    def down(s):
        return jnp.dot(act_ref[s], wd_ref[...], preferred_element_type=F32)

    def finish(s, ffn):
        x2 = x1_ref[s] + mod_ref[0, 5:6, :] * ffn
        ms = jnp.mean(x2 * x2, axis=-1, keepdims=True)
        out_ref[0, rows_of(s), :] = x2 * lax.rsqrt(ms + EPS) * fnw_ref[...]

    token_mix(0, mix_inputs(0))
    ffn_prev = None
    for s in range(OUT_STREAMS):
        fillers = {}
        if s + 1 < OUT_STREAMS:
            nxt = {}
            fillers[2] = lambda s=s, nxt=nxt: nxt.update(y=mix_inputs(s + 1))
            fillers[5] = lambda s=s, nxt=nxt: token_mix(s + 1, nxt["y"])
        if ffn_prev is not None:
            fillers[1] = lambda s=s, ffn_prev=ffn_prev: finish(s - 1, ffn_prev)
        activations(s, fillers)
        ffn_prev = down(s)
    finish(OUT_STREAMS - 1, ffn_prev)


def _out(x, o_f, o_b, z, att, mod, gn_w, an_w, n2_w, fn_w, wo, wg, wu, wd):
    b, s, d = x.shape

    def tok(width, col=0):
        return pl.BlockSpec((1, TM_OUT, width), lambda bi, i: (bi, i, col))

    def const(shape):
        return pl.BlockSpec(shape, lambda bi, i: (0,) * len(shape), pipeline_mode=pl.Buffered(1))

    return pl.pallas_call(
        _out_kernel,
        grid=(b, s // TM_OUT),
        in_specs=[tok(d), tok(HG_WIDTH), tok(HG_WIDTH), tok(HG_WIDTH, COL_HG), tok(ATT_WIDTH),
                  pl.BlockSpec((1, 6, d), lambda bi, i: (bi, 0, 0)),
                  const((1, HG_WIDTH)), const((1, ATT_WIDTH)), const((1, d)), const((1, d)),
                  const(wo.shape), const(wg.shape), const(wu.shape), const(wd.shape)],
        out_specs=tok(d),
        out_shape=jax.ShapeDtypeStruct((b, s, d), F32),
        scratch_shapes=[pltpu.VMEM((OUT_STREAMS, TM_OUT // OUT_STREAMS, d), F32),
                        pltpu.VMEM((OUT_STREAMS, TM_OUT // OUT_STREAMS, d), BF16),
                        pltpu.VMEM((OUT_STREAMS, TM_OUT // OUT_STREAMS, D_FF), BF16)],
        compiler_params=_cparams(("arbitrary", "arbitrary")),
        name="out_ffn",
    )(x, o_f, o_b, z, att, mod, gn_w, an_w, n2_w, fn_w, wo, wg, wu, wd)


def _cumsum_matrices():
    t = np.arange(HG_CHUNK)
    fwd = (t[None, :] <= t[:, None]).astype(np.float32)
    bwd = (t[None, :] >= t[:, None]).astype(np.float32)
    return jnp.asarray(np.stack([fwd, bwd]), dtype=BF16)


def kernel(x, c, rel_bias, w_ada, b_ada, norm1_w, w_in, hg_lower_bound, hg_norm_w, attn_norm_w,
           w_out, norm2_w, w_gate, w_up, w_down, final_norm_w):
    b, s, d = x.shape
    assert d == D_MODEL and s % ATT_TILE == 0 and s % HG_BLOCK == 0 and w_ada.shape[0] == 1
    c_pad = jnp.zeros((16, d), F32).at[:b].set(c)
    mod = _ada(c_pad, w_ada[0], b_ada[0][None, :])[:b].reshape(b, 6, d)

    bias = _bias_tiles(rel_bias.astype(F32), jnp.asarray(_bucket_tiles(ATT_GQ)))
    bias = bias.reshape(len(DILATIONS), ATT_PAIRS, 2 * ATT_GQ, ATT_GQ + 2 * HALF_WINDOW)
    z = _inproj(x, mod, norm1_w[0][None, :], w_in[0].astype(BF16))
    o_f, o_b = _hgrn(z, hg_lower_bound.astype(F32), _cumsum_matrices())
    att = _attn(z, bias)
    return _out(x, o_f, o_b, z, att, mod,
                hg_norm_w[0][None, :], attn_norm_w[0][None, :], norm2_w[0][None, :],
                final_norm_w[None, :],
                w_out[0].astype(BF16), w_gate[0].astype(BF16), w_up[0].astype(BF16),
                w_down[0].astype(BF16))
```
